```python
import math
import jax, jax.numpy as jnp
from jax import lax
import numpy as np

D_MODEL = 1024
BATCH = 16
SEQ = 2048
DEPTH = 4

N_A_LAYERS = DEPTH // 2
N_B_LAYERS = DEPTH - N_A_LAYERS
LRU_WIDTH = D_MODEL
LRU_BLOCKS = 8
LRU_BW = LRU_WIDTH // LRU_BLOCKS
CONV_WIDTH = 4
LRU_C = 8.0
N_HEADS = 16
HEAD_DIM = D_MODEL // N_HEADS
N_KV = 4
HPG = N_HEADS // N_KV
ROT_DIM = HEAD_DIM // 4
ROPE_THETA = 500000.0
CMP_BLOCK = 32
CMP_STRIDE = 16
CMP_HIDDEN = 4 * HEAD_DIM
SEL_BLOCK = 64
N_SEL = 8
WINDOW = 256
WIN_QBLOCK = 128
SEL_QCHUNK = 32
D_FF = 2816
N_EXPERTS = 8
TOP_K = 2
MOE_D_FF = 3584
DN_ALPHA = (2.0 * DEPTH) ** 0.25
DN_BETA = (8.0 * DEPTH) ** -0.25
LN_EPS = 1e-5
NEG = -1e30
BIG = 1e30
F32 = jnp.float32

kernel_name = 'hybrid_rglru_nsa_yoco_moe_deepnorm'


def layer_norm(x, g, b):
    xf = x.astype(F32)
    mu = jnp.mean(xf, -1, keepdims=True)
    var = jnp.mean(jnp.square(xf - mu), -1, keepdims=True)
    return ((xf - mu) * lax.rsqrt(var + LN_EPS) * g + b).astype(x.dtype)


def post_norm(x, y, g, b):
    return layer_norm(DN_ALPHA * x + y, g, b)


def partial_rope(x, pos):
    half = ROT_DIM // 2
    inv = ROPE_THETA ** (-jnp.arange(half, dtype=F32) / half)
    ang = pos.astype(F32)[:, None] * inv[None]
    shape = (1, x.shape[1]) + (1,) * (x.ndim - 3) + (half,)
    cos = jnp.cos(ang).reshape(shape)
    sin = jnp.sin(ang).reshape(shape)
    xr = x[..., :ROT_DIM].astype(F32)
    x1, x2 = xr[..., :half], xr[..., half:]
    rot = jnp.concatenate([x1 * cos - x2 * sin, x2 * cos + x1 * sin], -1).astype(x.dtype)
    return jnp.concatenate([rot, x[..., ROT_DIM:]], -1)


def causal_conv(x, w, b):
    S = x.shape[1]
    xp = jnp.pad(x, ((0, 0), (CONV_WIDTH - 1, 0), (0, 0)))
    y = b + w[0] * xp[:, 0:S]
    for k in range(1, CONV_WIDTH):
        y = y + w[k] * xp[:, k:k + S]
    return y


def recurrent_block(x, w_in, conv_w, conv_b, w_a, b_a, w_i, b_i, lam, w_out):
    B_, S, _ = x.shape
    u = x @ w_in
    gate = jax.nn.gelu(u[..., :LRU_WIDTH])
    xr = causal_conv(u[..., LRU_WIDTH:], conv_w, conv_b)
    xb = xr.reshape(B_, S, LRU_BLOCKS, LRU_BW)
    r = jax.nn.sigmoid(jnp.einsum('bsnc,ncd->bsnd', xb, w_a) + b_a).reshape(B_, S, LRU_WIDTH)
    i = jax.nn.sigmoid(jnp.einsum('bsnc,ncd->bsnd', xb, w_i) + b_i).reshape(B_, S, LRU_WIDTH)
    log_a = (-LRU_C * r.astype(F32)) * jax.nn.softplus(-lam.astype(F32))
    a = jnp.exp(log_a)
    b_t = jnp.sqrt(-jnp.expm1(2.0 * log_a)) * (i * xr).astype(F32)

    def combine(lhs, rhs):
        a1, h1 = lhs
        a2, h2 = rhs
        return a1 * a2, a2 * h1 + h2

    _, h = lax.associative_scan(combine, (a, b_t), axis=1)
    return (h.astype(x.dtype) * gate) @ w_out


def swiglu(x, w_gu, w_down):
    g, u = jnp.split(x @ w_gu, 2, axis=-1)
    return (jax.nn.silu(g) * u) @ w_down


def moe_swiglu(x, w_router, w_gu, w_down):
    logits = (x @ w_router).astype(F32)
    top_v, top_i = lax.top_k(logits, TOP_K)
    top_w = jax.nn.softmax(top_v, -1)
    gates = jnp.sum(jax.nn.one_hot(top_i, N_EXPERTS, dtype=F32) * top_w[..., None], -2)
    out = jnp.zeros_like(x)
    for e in range(N_EXPERTS):
        out = out + gates[..., e:e + 1].astype(x.dtype) * swiglu(x, w_gu[e], w_down[e])
    return out


def nsa_shared_kv(h, w_kv, cmp_pos, cmp_w1, cmp_w2):
    B_, S, _ = h.shape
    kv = (h @ w_kv).reshape(B_, S, 6, N_KV, HEAD_DIM)
    pos = jnp.arange(S)
    k_slc = partial_rope(kv[:, :, 2], pos)
    v_slc = kv[:, :, 3]
    k_win = partial_rope(kv[:, :, 4], pos)
    v_win = kv[:, :, 5]
    nc = (S - CMP_BLOCK) // CMP_STRIDE + 1
    idx = jnp.arange(nc)[:, None] * CMP_STRIDE + jnp.arange(CMP_BLOCK)[None]

    def compress(t, pe, w1, w2):
        blk = t[:, idx] + pe[None, None, :, None, :]
        blk = blk.transpose(0, 1, 3, 2, 4).reshape(B_, nc, N_KV, CMP_BLOCK * HEAD_DIM)
        return jax.nn.gelu(blk @ w1) @ w2

    k_cmp = compress(kv[:, :, 0], cmp_pos[0], cmp_w1[0], cmp_w2[0])
    v_cmp = compress(kv[:, :, 1], cmp_pos[1], cmp_w1[1], cmp_w2[1])
    return k_cmp, v_cmp, k_slc, v_slc, k_win, v_win


def cmp_branch(q, k_cmp, v_cmp, pos):
    nc = k_cmp.shape[1]
    s = jnp.einsum('bsghd,bcgd->bghsc', q, k_cmp).astype(F32)
    blk_end = jnp.arange(nc) * CMP_STRIDE + CMP_BLOCK - 1
    valid = blk_end[None, :] <= pos[:, None]
    p = jax.nn.softmax(jnp.where(valid, s, NEG), -1) * jnp.any(valid, -1)[:, None].astype(F32)
    o = jnp.einsum('bghsc,bcgd->bsghd', p.astype(v_cmp.dtype), v_cmp)
    return o, p


def select_blocks(p_cmp, pos):
    S = pos.shape[0]
    nb = S // SEL_BLOCK
    nc = p_cmp.shape[-1]
    c_start = jnp.arange(nc) * CMP_STRIDE
    j_start = jnp.arange(nb) * SEL_BLOCK
    overlap = ((c_start[:, None] < j_start[None] + SEL_BLOCK)
               & (c_start[:, None] + CMP_BLOCK > j_start[None])).astype(F32)
    imp = jnp.einsum('bghsc,cn->bsgn', p_cmp, overlap)
    cur = pos // SEL_BLOCK
    jj = jnp.arange(nb)
    forced = (jj[None] == 0) | (jj[None] == cur[:, None]) | (jj[None] == cur[:, None] - 1)
    causal = j_start[None] <= pos[:, None]
    score = jnp.where(forced[None, :, None, :], BIG,
                      jnp.where(causal[None, :, None, :], imp, NEG))
    _, idx = lax.top_k(score, min(N_SEL, nb))
    return idx


def slc_branch(q, k_slc, v_slc, idx, pos):
    B_, S = q.shape[:2]
    nb = S // SEL_BLOCK
    kb = k_slc.reshape(B_, nb, SEL_BLOCK, N_KV, HEAD_DIM).transpose(0, 3, 1, 2, 4)
    vb = v_slc.reshape(B_, nb, SEL_BLOCK, N_KV, HEAD_DIM).transpose(0, 3, 1, 2, 4)
    nq = S // SEL_QCHUNK
    kk = idx.shape[-1]
    bi = jnp.arange(B_)[:, None, None, None]
    gi = jnp.arange(N_KV)[None, None, :, None]

    def chunk(args):
        q_c, idx_c, t_c = args
        k_g = kb[bi, gi, idx_c]
        v_g = vb[bi, gi, idx_c]
        s = jnp.einsum('bqghd,bqgksd->bqghks', q_c, k_g).astype(F32)
        kpos = idx_c[..., None] * SEL_BLOCK + jnp.arange(SEL_BLOCK)
        mask = (kpos <= t_c[None, :, None, None, None])[:, :, :, None]
        s = jnp.where(mask, s, NEG)
        p = jax.nn.softmax(s.reshape(s.shape[:4] + (kk * SEL_BLOCK,)), -1).reshape(s.shape)
        return jnp.einsum('bqghks,bqgksd->bqghd', p.astype(v_g.dtype), v_g)

    qs = q.reshape(B_, nq, SEL_QCHUNK, N_KV, HPG, HEAD_DIM).swapaxes(0, 1)
    ids = idx.reshape(B_, nq, SEL_QCHUNK, N_KV, kk).swapaxes(0, 1)
    ts = pos.reshape(nq, SEL_QCHUNK)
    o = lax.map(chunk, (qs, ids, ts))
    return o.swapaxes(0, 1).reshape(B_, S, N_KV, HPG, HEAD_DIM)


def win_branch(q, k, v):
    B_, S = q.shape[:2]
    nqb = S // WIN_QBLOCK
    nback = WINDOW // WIN_QBLOCK

    def bands(t):
        tp = jnp.pad(t, ((0, 0), (WINDOW, 0), (0, 0), (0, 0)))
        tp = tp.reshape(B_, nqb + nback, WIN_QBLOCK, N_KV, HEAD_DIM)
        return jnp.concatenate([tp[:, j:j + nqb] for j in range(nback + 1)], axis=2)

    kw, vw = bands(k), bands(v)
    qb = q.reshape(B_, nqb, WIN_QBLOCK, N_KV, HPG, HEAD_DIM)
    s = jnp.einsum('bnighd,bnjgd->bnghij', qb, kw).astype(F32)
    i = jnp.arange(WIN_QBLOCK)[:, None]
    j = jnp.arange(WINDOW + WIN_QBLOCK)[None]
    rel = i + WINDOW - j
    kpos = jnp.arange(nqb)[:, None, None] * WIN_QBLOCK - WINDOW + j
    mask = (rel >= 0) & (rel < WINDOW) & (kpos >= 0)
    p = jax.nn.softmax(jnp.where(mask[None, :, None, None], s, NEG), -1)
    o = jnp.einsum('bnghij,bnjgd->bnighd', p.astype(vw.dtype), vw)
    return o.reshape(B_, S, N_KV, HPG, HEAD_DIM)


def nsa_mixer(x, w_qg, w_o, k_cmp, v_cmp, k_slc, v_slc, k_win, v_win):
    B_, S, _ = x.shape
    qg = x @ w_qg
    q = qg[..., :N_HEADS * HEAD_DIM].reshape(B_, S, N_KV, HPG, HEAD_DIM) * (HEAD_DIM ** -0.5)
    gates = jax.nn.sigmoid(qg[..., N_HEADS * HEAD_DIM:].astype(F32)).reshape(B_, S, N_KV, HPG, 3)
    pos = jnp.arange(S)
    q_rot = partial_rope(q, pos)
    o_cmp, p_cmp = cmp_branch(q, k_cmp, v_cmp, pos)
    idx = select_blocks(p_cmp, pos)
    o_slc = slc_branch(q_rot, k_slc, v_slc, idx, pos)
    o_win = win_branch(q_rot, k_win, v_win)
    o = (gates[..., 0:1] * o_cmp + gates[..., 1:2] * o_slc + gates[..., 2:3] * o_win).astype(x.dtype)
    return o.reshape(B_, S, N_HEADS * HEAD_DIM) @ w_o


def setup_inputs(seed: int = 0) -> dict:
    key = jax.random.key(seed)
    ks = jax.random.split(key, 26)

    def nrm(k, shape, scale):
        return jax.random.normal(k, shape, F32) * scale

    nA, nB = N_A_LAYERS, N_B_LAYERS
    n_dense = (DEPTH + 1) // 2
    n_moe = DEPTH // 2
    u = jax.random.uniform(ks[9], (nA, LRU_WIDTH), F32, 0.9, 0.999)
    s = u ** (1.0 / LRU_C)
    lam = jnp.log(s) - jnp.log1p(-s)
    return {
        'x': nrm(ks[0], (BATCH, SEQ, D_MODEL), 1.0),
        'ln_g': 1.0 + nrm(ks[1], (DEPTH, 2, D_MODEL), 0.02),
        'ln_b': nrm(ks[2], (DEPTH, 2, D_MODEL), 0.02),
        'lru_w_in': nrm(ks[3], (nA, D_MODEL, 2 * LRU_WIDTH), D_MODEL ** -0.5),
        'lru_conv_w': nrm(ks[4], (nA, CONV_WIDTH, LRU_WIDTH), CONV_WIDTH ** -0.5),
        'lru_conv_b': nrm(ks[5], (nA, LRU_WIDTH), 0.01),
        'lru_w_a': nrm(ks[6], (nA, LRU_BLOCKS, LRU_BW, LRU_BW), LRU_BW ** -0.5),
        'lru_b_a': nrm(ks[7], (nA, LRU_BLOCKS, LRU_BW), 0.01),
        'lru_w_i': nrm(ks[8], (nA, LRU_BLOCKS, LRU_BW, LRU_BW), LRU_BW ** -0.5),
        'lru_b_i': nrm(ks[10], (nA, LRU_BLOCKS, LRU_BW), 0.01),
        'lru_lambda': lam,
        'lru_w_out': nrm(ks[11], (nA, LRU_WIDTH, D_MODEL), LRU_WIDTH ** -0.5 * DN_BETA),
        'nsa_w_kv': nrm(ks[12], (D_MODEL, 6 * N_KV * HEAD_DIM), D_MODEL ** -0.5),
        'nsa_cmp_pos': nrm(ks[13], (2, CMP_BLOCK, HEAD_DIM), 0.1),
        'nsa_cmp_w1': nrm(ks[14], (2, CMP_BLOCK * HEAD_DIM, CMP_HIDDEN), (CMP_BLOCK * HEAD_DIM) ** -0.5),
        'nsa_cmp_w2': nrm(ks[15], (2, CMP_HIDDEN, HEAD_DIM), CMP_HIDDEN ** -0.5),
        'nsa_w_qg': nrm(ks[16], (nB, D_MODEL, N_HEADS * HEAD_DIM + 3 * N_HEADS), D_MODEL ** -0.5),
        'nsa_w_o': nrm(ks[17], (nB, N_HEADS * HEAD_DIM, D_MODEL), (N_HEADS * HEAD_DIM) ** -0.5 * DN_BETA),
        'ffn_w_gu': nrm(ks[18], (n_dense, D_MODEL, 2 * D_FF), D_MODEL ** -0.5),
        'ffn_w_down': nrm(ks[19], (n_dense, D_FF, D_MODEL), D_FF ** -0.5 * DN_BETA),
        'moe_w_router': nrm(ks[20], (n_moe, D_MODEL, N_EXPERTS), D_MODEL ** -0.5),
        'moe_w_gu': nrm(ks[21], (n_moe, N_EXPERTS, D_MODEL, 2 * MOE_D_FF), D_MODEL ** -0.5),
        'moe_w_down': nrm(ks[22], (n_moe, N_EXPERTS, MOE_D_FF, D_MODEL), MOE_D_FF ** -0.5 * DN_BETA),
    }


def reference(x, ln_g, ln_b, lru_w_in, lru_conv_w, lru_conv_b, lru_w_a, lru_b_a, lru_w_i, lru_b_i,
              lru_lambda, lru_w_out, nsa_w_kv, nsa_cmp_pos, nsa_cmp_w1, nsa_cmp_w2, nsa_w_qg, nsa_w_o,
              ffn_w_gu, ffn_w_down, moe_w_router, moe_w_gu, moe_w_down):
    h = x
    shared_kv = None
    for l in range(DEPTH):
        if l < N_A_LAYERS:
            y = recurrent_block(h, lru_w_in[l], lru_conv_w[l], lru_conv_b[l], lru_w_a[l], lru_b_a[l],
                                lru_w_i[l], lru_b_i[l], lru_lambda[l], lru_w_out[l])
        else:
            lb = l - N_A_LAYERS
            y = nsa_mixer(h, nsa_w_qg[lb], nsa_w_o[lb], *shared_kv)
        h = post_norm(h, y, ln_g[l, 0], ln_b[l, 0])
        if l % 2 == 0:
            f = swiglu(h, ffn_w_gu[l // 2], ffn_w_down[l // 2])
        else:
            f = moe_swiglu(h, moe_w_router[l // 2], moe_w_gu[l // 2], moe_w_down[l // 2])
        h = post_norm(h, f, ln_g[l, 1], ln_b[l, 1])
        if l == N_A_LAYERS - 1:
            shared_kv = nsa_shared_kv(h, nsa_w_kv, nsa_cmp_pos, nsa_cmp_w1, nsa_cmp_w2)
    return h
```

```python
import functools

import jax
import jax.numpy as jnp
from jax import lax
from jax.experimental import pallas as pl
from jax.experimental.pallas import tpu as pltpu

F32 = jnp.float32
BF16 = jnp.bfloat16
I32 = jnp.int32

DEPTH = 4
N_A_LAYERS = DEPTH // 2
LRU_BLOCKS = 8
CONV_WIDTH = 4
LRU_C = 8.0
N_HEADS = 16
N_KV = 4
HPG = N_HEADS // N_KV
HEAD_DIM = 64
ROT_DIM = HEAD_DIM // 4
ROPE_THETA = 500000.0
CMP_BLOCK = 32
CMP_STRIDE = 16
SEL_BLOCK = 64
N_SEL = 8
WINDOW = 256
N_EXPERTS = 8
DN_ALPHA = (2.0 * DEPTH) ** 0.25
LN_EPS = 1e-5
NEG = -1e30
BIG = 1e30

LANES = 128
SUBLANES = 8
VMEM_LIMIT = 52 * 1024 * 1024

LRU_TS = 256
FFN_TM = 1024
ROUTE_TM = 512
ROW_TM = 256
PROJ_TM = 512
ATT_TQ = 128
SLC_KC = 256


def _cparams(sem):
    return pltpu.CompilerParams(dimension_semantics=sem, vmem_limit_bytes=VMEM_LIMIT)


def _dot(a, b):
    return jnp.dot(a, b, preferred_element_type=F32)


def _split(a):
    hi = a.astype(BF16)
    lo = (a - hi.astype(F32)).astype(BF16)
    return hi, lo


def _dot_f32(a, b):
    a_hi, a_lo = _split(a)
    b_hi, b_lo = _split(b)
    return _dot(a_hi, b_hi) + (_dot(a_hi, b_lo) + _dot(a_lo, b_hi))


def _postnorm(res, y, g, b):
    z = DN_ALPHA * res + y
    mu = jnp.mean(z, axis=-1, keepdims=True)
    zc = z - mu
    var = jnp.mean(zc * zc, axis=-1, keepdims=True)
    return zc * lax.rsqrt(var + LN_EPS) * g + b


def _lru_kernel(x_ref, win_ref, cw_ref, cb_ref, wai_ref, ba_ref, bi_ref, lam_ref, wout_ref,
                g_ref, b_ref, o_ref, hcar_ref, xprev_ref, hg_ref, *, ts, width, nblk):
    bw = width // nblk

    @pl.when(pl.program_id(1) == 0)
    def _():
        hcar_ref[...] = jnp.zeros_like(hcar_ref)
        xprev_ref[...] = jnp.zeros_like(xprev_ref)

    x = x_ref[0]
    u = _dot(x.astype(BF16), win_ref[...])
    xc = u[:, width:]
    xcat = jnp.concatenate([xprev_ref[...], xc], axis=0)
    xr = cb_ref[...] + cw_ref[CONV_WIDTH - 1:CONV_WIDTH, :] * xc
    for s in range(1, CONV_WIDTH):
        xr = xr + cw_ref[CONV_WIDTH - 1 - s:CONV_WIDTH - s, :] * xcat[SUBLANES - s:SUBLANES - s + ts]
    xprev_ref[...] = xc[ts - SUBLANES:ts]

    row = lax.broadcasted_iota(I32, (ts, bw), 0)
    for n in range(nblk):
        sl = slice(n * bw, (n + 1) * bw)
        xb = xr[:, sl]
        ri = _dot(xb.astype(BF16), wai_ref[n])
        r = jax.nn.sigmoid(ri[:, :bw] + ba_ref[:, sl])
        i = jax.nn.sigmoid(ri[:, bw:] + bi_ref[:, sl])
        lam = lam_ref[:, sl]
        sp = jnp.maximum(-lam, 0.0) + jnp.log(1.0 + jnp.exp(-jnp.abs(lam)))
        log_a = (-LRU_C * r) * sp
        a = jnp.exp(log_a)
        th = jnp.tanh(log_a)
        h = jnp.sqrt(-2.0 * th / (1.0 - th)) * (i * xb)
        d = 1
        while d < ts:
            keep = row >= d
            a_sh = jnp.where(keep, pltpu.roll(a, d, 0), 1.0)
            h_sh = jnp.where(keep, pltpu.roll(h, d, 0), 0.0)
            h = a * h_sh + h
            a = a * a_sh
            d *= 2
        h = h + a * hcar_ref[:, sl]
        hcar_ref[:, sl] = h[ts - 1:ts]
        gate = jax.nn.gelu(u[:, sl])
        hg_ref[:, sl] = (h * gate).astype(BF16)

    y = _dot(hg_ref[...], wout_ref[...])
    o_ref[0] = _postnorm(x, y, g_ref[...], b_ref[...])


def _lru_layer(x, w_in, conv_w, conv_b, w_a, b_a, w_i, b_i, lam, w_out, g, b):
    bsz, seq, d = x.shape
    width = w_out.shape[0]
    nblk = w_a.shape[0]
    ts = min(LRU_TS, seq)
    wai = jnp.concatenate([w_a, w_i], axis=-1).astype(BF16)
    row2 = lambda v: v.reshape(1, -1)
    const2 = lambda bi, ti: (0, 0)
    kern = functools.partial(_lru_kernel, ts=ts, width=width, nblk=nblk)
    return pl.pallas_call(
        kern,
        grid=(bsz, seq // ts),
        in_specs=[
            pl.BlockSpec((1, ts, d), lambda bi, ti: (bi, ti, 0)),
            pl.BlockSpec((d, 2 * width), const2),
            pl.BlockSpec((CONV_WIDTH, width), const2),
            pl.BlockSpec((1, width), const2),
            pl.BlockSpec(wai.shape, lambda bi, ti: (0, 0, 0)),
            pl.BlockSpec((1, width), const2),
            pl.BlockSpec((1, width), const2),
            pl.BlockSpec((1, width), const2),
            pl.BlockSpec((width, d), const2),
            pl.BlockSpec((1, d), const2),
            pl.BlockSpec((1, d), const2),
        ],
        out_specs=pl.BlockSpec((1, ts, d), lambda bi, ti: (bi, ti, 0)),
        out_shape=jax.ShapeDtypeStruct((bsz, seq, d), F32),
        scratch_shapes=[
            pltpu.VMEM((1, width), F32),
            pltpu.VMEM((SUBLANES, width), F32),
            pltpu.VMEM((ts, width), BF16),
        ],
        compiler_params=_cparams(("parallel", "arbitrary")),
        name="lru_layer",
    )(x, w_in.astype(BF16), conv_w, row2(conv_b), wai, row2(b_a), row2(b_i), row2(lam),
      w_out.astype(BF16), row2(g), row2(b))


def _ffn_kernel(te_ref, nu_ref, x_ref, wg_ref, wu_ref, wd_ref, g_ref, b_ref, o_ref, xb_ref, acc_ref,
                *, nf, postnorm):
    i = pl.program_id(0)
    f = pl.program_id(1)
    used = i < nu_ref[0]

    @pl.when(used)
    def _():
        @pl.when(f == 0)
        def _():
            xb_ref[...] = x_ref[...].astype(BF16)
            acc_ref[...] = jnp.zeros_like(acc_ref)

        xb = xb_ref[...]
        gp = _dot(xb, wg_ref[0])
        up = _dot(xb, wu_ref[0])
        act = (gp * jax.nn.sigmoid(gp) * up).astype(BF16)
        acc_ref[...] += _dot(act, wd_ref[0])

        @pl.when(f == nf - 1)
        def _():
            if postnorm:
                o_ref[...] = _postnorm(x_ref[...], acc_ref[...], g_ref[...], b_ref[...])
            else:
                o_ref[...] = acc_ref[...]

    @pl.when(jnp.logical_and(jnp.logical_not(used), f == nf - 1))
    def _():
        o_ref[...] = jnp.zeros_like(o_ref)


def _ffn(x, w_gu, w_down, tile_expert, n_used, g, b, *, postnorm, tf):
    rows, d = x.shape
    ff = w_down.shape[1]
    tm = min(FFN_TM, rows)
    nf = ff // tf
    nt = rows // tm

    def fidx(i, f, nu):
        return jnp.where(i < nu[0], f, nf - 1)

    kern = functools.partial(_ffn_kernel, nf=nf, postnorm=postnorm)
    grid_spec = pltpu.PrefetchScalarGridSpec(
        num_scalar_prefetch=2,
        grid=(nt, nf),
        in_specs=[
            pl.BlockSpec((tm, d), lambda i, f, te, nu: (i, 0)),
            pl.BlockSpec((1, d, tf), lambda i, f, te, nu: (te[i], 0, fidx(i, f, nu))),
            pl.BlockSpec((1, d, tf), lambda i, f, te, nu: (te[i], 0, fidx(i, f, nu) + nf)),
            pl.BlockSpec((1, tf, d), lambda i, f, te, nu: (te[i], fidx(i, f, nu), 0)),
            pl.BlockSpec((1, d), lambda i, f, te, nu: (0, 0)),
            pl.BlockSpec((1, d), lambda i, f, te, nu: (0, 0)),
        ],
        out_specs=pl.BlockSpec((tm, d), lambda i, f, te, nu: (i, 0)),
        scratch_shapes=[pltpu.VMEM((tm, d), BF16), pltpu.VMEM((tm, d), F32)],
    )
    return pl.pallas_call(
        kern,
        grid_spec=grid_spec,
        out_shape=jax.ShapeDtypeStruct((rows, d), F32),
        compiler_params=_cparams(("parallel", "arbitrary")),
        name="ffn_postnorm" if postnorm else "ffn_experts",
    )(tile_expert, n_used, x, w_gu, w_gu, w_down, g.reshape(1, -1), b.reshape(1, -1))


def _dense_ffn(h2, w_gu, w_down, g, b):
    rows = h2.shape[0]
    nt = rows // min(FFN_TM, rows)
    return _ffn(h2, w_gu[None].astype(BF16), w_down[None].astype(BF16),
                jnp.zeros((nt,), I32), jnp.full((1,), nt, I32), g, b, postnorm=True, tf=256)


def _router_kernel(h_ref, wr_ref, mi_ref, mw_ref, cnt_ref, run_ref, *, tm, n_exp):
    @pl.when(pl.program_id(0) == 0)
    def _():
        run_ref[...] = jnp.zeros_like(run_ref)

    logits = _dot_f32(h_ref[...], wr_ref[...])
    lane = lax.broadcasted_iota(I32, (tm, LANES), 1)
    lg = jnp.where(lane < n_exp, logits, -jnp.inf)
    m0 = jnp.max(lg, axis=1, keepdims=True)
    i0 = jnp.min(jnp.where(lg == m0, lane, LANES), axis=1, keepdims=True)
    lg1 = jnp.where(lane == i0, -jnp.inf, lg)
    m1 = jnp.max(lg1, axis=1, keepdims=True)
    i1 = jnp.min(jnp.where(lg1 == m1, lane, LANES), axis=1, keepdims=True)
    e1 = jnp.exp(m1 - m0)
    w0 = 1.0 / (1.0 + e1)
    w1 = e1 / (1.0 + e1)

    oh0 = lane == i0
    oh1 = lane == i1
    oh = jnp.logical_or(oh0, oh1).astype(F32)
    tri = (lax.broadcasted_iota(I32, (tm, tm), 0) > lax.broadcasted_iota(I32, (tm, tm), 1)).astype(BF16)
    before = _dot(tri, oh.astype(BF16)) + run_ref[0:1, :]
    r0 = jnp.sum(jnp.where(oh0, before, 0.0), axis=1, keepdims=True).astype(I32)
    r1 = jnp.sum(jnp.where(oh1, before, 0.0), axis=1, keepdims=True).astype(I32)
    run_ref[0:1, :] = run_ref[0:1, :] + jnp.sum(oh, axis=0, keepdims=True)

    mi_ref[...] = jnp.where(lane == 0, i0, jnp.where(lane == 1, i1, jnp.where(lane == 2, r0, r1)))
    mw_ref[...] = jnp.where(lane == 0, w0, w1)
    cnt_ref[...] = run_ref[...]


def _router(h2, w_router):
    rows, d = h2.shape
    n_exp = w_router.shape[1]
    tm = min(ROUTE_TM, rows)
    wr = jnp.pad(w_router, ((0, 0), (0, LANES - n_exp)))
    kern = functools.partial(_router_kernel, tm=tm, n_exp=n_exp)
    return pl.pallas_call(
        kern,
        grid=(rows // tm,),
        in_specs=[pl.BlockSpec((tm, d), lambda i: (i, 0)), pl.BlockSpec((d, LANES), lambda i: (0, 0))],
        out_specs=[
            pl.BlockSpec((tm, LANES), lambda i: (i, 0)),
            pl.BlockSpec((tm, LANES), lambda i: (i, 0)),
            pl.BlockSpec((SUBLANES, LANES), lambda i: (0, 0)),
        ],
        out_shape=[
            jax.ShapeDtypeStruct((rows, LANES), I32),
            jax.ShapeDtypeStruct((rows, LANES), F32),
            jax.ShapeDtypeStruct((SUBLANES, LANES), F32),
        ],
        scratch_shapes=[pltpu.VMEM((SUBLANES, LANES), F32)],
        compiler_params=_cparams(("arbitrary",)),
        name="moe_router",
    )(h2, wr)


def _row_copy(src, dst, sem):
    return pltpu.make_async_copy(src, dst, sem)


def _scatter_kernel(pos_ref, h_ref, zin_ref, xs_ref, sem, *, tm):
    del zin_ref

    def start(r, c):
        row = h_ref.at[pl.ds(r, 1)]
        _row_copy(row, xs_ref.at[pl.ds(pos_ref[0, 0, 2 * r], 1)], sem).start()
        _row_copy(row, xs_ref.at[pl.ds(pos_ref[0, 0, 2 * r + 1], 1)], sem).start()
        return c

    lax.fori_loop(0, tm, start, 0)

    def wait(r, c):
        row = h_ref.at[pl.ds(r, 1)]
        _row_copy(row, xs_ref.at[pl.ds(pos_ref[0, 0, 2 * r], 1)], sem).wait()
        _row_copy(row, xs_ref.at[pl.ds(pos_ref[0, 0, 2 * r + 1], 1)], sem).wait()
        return c

    lax.fori_loop(0, tm, wait, 0)


def _scatter_rows(h2, pos, n_sorted):
    rows, d = h2.shape
    tm = min(ROW_TM, rows)
    nt = rows // tm
    kern = functools.partial(_scatter_kernel, tm=tm)
    return pl.pallas_call(
        kern,
        grid=(nt,),
        in_specs=[
            pl.BlockSpec((1, 1, 2 * tm), lambda i: (i, 0, 0), memory_space=pltpu.SMEM),
            pl.BlockSpec((tm, d), lambda i: (i, 0)),
            pl.BlockSpec(memory_space=pl.ANY),
        ],
        out_specs=pl.BlockSpec(memory_space=pl.ANY),
        out_shape=jax.ShapeDtypeStruct((n_sorted, d), F32),
        scratch_shapes=[pltpu.SemaphoreType.DMA(())],
        input_output_aliases={2: 0},
        compiler_params=_cparams(("arbitrary",)),
        name="moe_scatter",
    )(pos.reshape(nt, 1, 2 * tm), h2, jnp.zeros((n_sorted, d), F32))


def _combine_kernel(pos_ref, mw_ref, h_ref, ys_ref, g_ref, b_ref, o_ref, y0_ref, y1_ref, sem, *, tm):
    def start(r, c):
        _row_copy(ys_ref.at[pl.ds(pos_ref[0, 0, 2 * r], 1)], y0_ref.at[pl.ds(r, 1)], sem).start()
        _row_copy(ys_ref.at[pl.ds(pos_ref[0, 0, 2 * r + 1], 1)], y1_ref.at[pl.ds(r, 1)], sem).start()
        return c

    lax.fori_loop(0, tm, start, 0)

    def wait(r, c):
        _row_copy(ys_ref.at[pl.ds(pos_ref[0, 0, 2 * r], 1)], y0_ref.at[pl.ds(r, 1)], sem).wait()
        _row_copy(ys_ref.at[pl.ds(pos_ref[0, 0, 2 * r + 1], 1)], y1_ref.at[pl.ds(r, 1)], sem).wait()
        return c

    lax.fori_loop(0, tm, wait, 0)
    w = mw_ref[...]
    y = w[:, 0:1] * y0_ref[...] + w[:, 1:2] * y1_ref[...]
    o_ref[...] = _postnorm(h_ref[...], y, g_ref[...], b_ref[...])


def _combine_rows(h2, ys, pos, mw, g, b):
    rows, d = h2.shape
    tm = min(ROW_TM, rows)
    nt = rows // tm
    kern = functools.partial(_combine_kernel, tm=tm)
    return pl.pallas_call(
        kern,
        grid=(nt,),
        in_specs=[
            pl.BlockSpec((1, 1, 2 * tm), lambda i: (i, 0, 0), memory_space=pltpu.SMEM),
            pl.BlockSpec((tm, LANES), lambda i: (i, 0)),
            pl.BlockSpec((tm, d), lambda i: (i, 0)),
            pl.BlockSpec(memory_space=pl.ANY),
            pl.BlockSpec((1, d), lambda i: (0, 0)),
            pl.BlockSpec((1, d), lambda i: (0, 0)),
        ],
        out_specs=pl.BlockSpec((tm, d), lambda i: (i, 0)),
        out_shape=jax.ShapeDtypeStruct((rows, d), F32),
        scratch_shapes=[pltpu.VMEM((tm, d), F32), pltpu.VMEM((tm, d), F32), pltpu.SemaphoreType.DMA(())],
        compiler_params=_cparams(("arbitrary",)),
        name="moe_combine",
    )(pos.reshape(nt, 1, 2 * tm), mw, h2, ys, g.reshape(1, -1), b.reshape(1, -1))


def _moe(h2, w_router, w_gu, w_down, g, b):
    rows, d = h2.shape
    n_exp = w_router.shape[1]
    tm = min(FFN_TM, rows)
    mi, mw, cnt = _router(h2, w_router)
    counts = cnt[0, :n_exp].astype(I32)
    tiles_per = (counts + tm - 1) // tm
    tile_end = jnp.cumsum(tiles_per)
    row_start = (tile_end - tiles_per) * tm
    e01 = mi[:, 0:2]
    onehot = e01[:, :, None] == jnp.arange(n_exp, dtype=I32)[None, None, :]
    pos = jnp.sum(jnp.where(onehot, row_start[None, None, :], 0), axis=-1) + mi[:, 2:4]
    n_tiles = (2 * rows) // tm + n_exp
    n_used = tile_end[-1]
    tid = jnp.minimum(jnp.arange(n_tiles, dtype=I32), n_used - 1)
    tile_expert = jnp.sum((tid[:, None] >= tile_end[None, :]).astype(I32), axis=1)
    tile_expert = jnp.minimum(tile_expert, n_exp - 1)

    xs = _scatter_rows(h2, pos, n_tiles * tm)
    ys = _ffn(xs, w_gu.astype(BF16), w_down.astype(BF16), tile_expert, n_used.reshape(1), g, b,
              postnorm=False, tf=512)
    return _combine_rows(h2, ys, pos, mw, g, b)


def _rope_cols(x, c, sa, sb):
    half = ROT_DIM // 2
    return x * c + pltpu.roll(x, LANES - half, 1) * sa + pltpu.roll(x, half, 1) * sb


def _rope_tables(seq):
    half = ROT_DIM // 2
    inv = ROPE_THETA ** (-jnp.arange(half, dtype=F32) / half)
    ang = jnp.arange(seq, dtype=F32)[:, None] * inv[None]
    cos, sin = jnp.cos(ang), jnp.sin(ang)
    ones = jnp.ones((seq, HEAD_DIM - ROT_DIM), F32)
    zeros = jnp.zeros((seq, HEAD_DIM - ROT_DIM), F32)
    zh = jnp.zeros((seq, half), F32)
    c = jnp.concatenate([cos, cos, ones], axis=1)
    sa = jnp.concatenate([-sin, zh, zeros], axis=1)
    sb = jnp.concatenate([zh, sin, zeros], axis=1)
    rep = LANES // HEAD_DIM
    return tuple(jnp.tile(t, (1, rep)) for t in (c, sa, sb))


def _kv_kernel(h_ref, w_ref, c_ref, sa_ref, sb_ref, o_ref, *, gw):
    res = _dot(h_ref[...].astype(BF16), w_ref[...])
    c, sa, sb = c_ref[...], sa_ref[...], sb_ref[...]
    for part in range(6):
        for j in range(gw // LANES):
            lo = part * gw + j * LANES
            x = res[:, lo:lo + LANES]
            if part in (2, 4):
                x = _rope_cols(x, c, sa, sb)
            o_ref[:, lo:lo + LANES] = x.astype(BF16)


def _kv_proj(h2, w_kv, tabs, seq):
    rows, d = h2.shape
    n = w_kv.shape[1]
    tm = min(PROJ_TM, seq)
    nsb = seq // tm
    tab_spec = pl.BlockSpec((tm, LANES), lambda i: (i % nsb, 0))
    kern = functools.partial(_kv_kernel, gw=n // 6)
    return pl.pallas_call(
        kern,
        grid=(rows // tm,),
        in_specs=[pl.BlockSpec((tm, d), lambda i: (i, 0)), pl.BlockSpec((d, n), lambda i: (0, 0)),
                  tab_spec, tab_spec, tab_spec],
        out_specs=pl.BlockSpec((tm, n), lambda i: (i, 0)),
        out_shape=jax.ShapeDtypeStruct((rows, n), BF16),
        compiler_params=_cparams(("parallel",)),
        name="kv_proj",
    )(h2, w_kv.astype(BF16), *tabs)


def _qg_kernel(h_ref, w_ref, c_ref, sa_ref, sb_ref, q_ref, qr_ref, gt_ref, *, nq):
    res = _dot(h_ref[...].astype(BF16), w_ref[...])
    c, sa, sb = c_ref[...], sa_ref[...], sb_ref[...]
    scale = HEAD_DIM ** -0.5
    for j in range(nq // LANES):
        x = res[:, j * LANES:(j + 1) * LANES] * scale
        q_ref[:, j * LANES:(j + 1) * LANES] = x.astype(BF16)
        qr_ref[:, j * LANES:(j + 1) * LANES] = _rope_cols(x, c, sa, sb).astype(BF16)
    gt_ref[...] = jax.nn.sigmoid(res[:, nq:nq + LANES])


def _qg_proj(h2, w_qg, tabs, seq):
    rows, d = h2.shape
    nq = N_HEADS * HEAD_DIM
    tm = min(PROJ_TM, seq)
    nsb = seq // tm
    w = jnp.pad(w_qg, ((0, 0), (0, nq + LANES - w_qg.shape[1]))).astype(BF16)
    tab_spec = pl.BlockSpec((tm, LANES), lambda i: (i % nsb, 0))
    kern = functools.partial(_qg_kernel, nq=nq)
    return pl.pallas_call(
        kern,
        grid=(rows // tm,),
        in_specs=[pl.BlockSpec((tm, d), lambda i: (i, 0)), pl.BlockSpec((d, nq + LANES), lambda i: (0, 0)),
                  tab_spec, tab_spec, tab_spec],
        out_specs=[pl.BlockSpec((tm, nq), lambda i: (i, 0)), pl.BlockSpec((tm, nq), lambda i: (i, 0)),
                   pl.BlockSpec((tm, LANES), lambda i: (i, 0))],
        out_shape=[jax.ShapeDtypeStruct((rows, nq), BF16), jax.ShapeDtypeStruct((rows, nq), BF16),
                   jax.ShapeDtypeStruct((rows, LANES), F32)],
        compiler_params=_cparams(("parallel",)),
        name="qg_proj",
    )(h2, w, *tabs)


def _mm_ln_kernel(a_ref, w_ref, res_ref, g_ref, b_ref, o_ref):
    y = _dot(a_ref[...], w_ref[...])
    o_ref[...] = _postnorm(res_ref[...], y, g_ref[...], b_ref[...])


def _mm_postnorm(a, w, res, g, b):
    rows, k = a.shape
    d = w.shape[1]
    tm = min(PROJ_TM, rows)
    return pl.pallas_call(
        _mm_ln_kernel,
        grid=(rows // tm,),
        in_specs=[pl.BlockSpec((tm, k), lambda i: (i, 0)), pl.BlockSpec((k, d), lambda i: (0, 0)),
                  pl.BlockSpec((tm, d), lambda i: (i, 0)),
                  pl.BlockSpec((1, d), lambda i: (0, 0)), pl.BlockSpec((1, d), lambda i: (0, 0))],
        out_specs=pl.BlockSpec((tm, d), lambda i: (i, 0)),
        out_shape=jax.ShapeDtypeStruct((rows, d), F32),
        compiler_params=_cparams(("parallel",)),
        name="out_proj_postnorm",
    )(a, w.astype(BF16), res, g.reshape(1, -1), b.reshape(1, -1))


def _compress_kernel(x_ref, w1_ref, pe_ref, w2_ref, o_ref, *, nb, rows_per):
    half = w1_ref.shape[1] // 2
    x = x_ref[0].reshape(nb * rows_per, half)
    first = _dot(x, w1_ref[0, :half, :])
    second = _dot(x, w1_ref[0, half:, :])
    bias = _dot(pe_ref[0].astype(BF16), w1_ref[0])[0:1, :]
    for j in range(nb):
        sl = slice(j * rows_per, (j + 1) * rows_per)
        hid = first[sl] + pltpu.roll(second[sl], rows_per - 1, 0) + bias
        o_ref[0, j] = _dot(jax.nn.gelu(hid).astype(BF16), w2_ref[0]).astype(BF16)


def _compress(xkv, w1, pe, w2):
    _, bg, rows_per, k2 = xkv.shape
    hid = w1.shape[2]
    dh = w2.shape[2]
    nb = 4 if bg % 4 == 0 else 1
    pe8 = jnp.broadcast_to(pe.reshape(2, 1, 2 * k2), (2, SUBLANES, 2 * k2))
    kern = functools.partial(_compress_kernel, nb=nb, rows_per=rows_per)
    return pl.pallas_call(
        kern,
        grid=(2, bg // nb),
        in_specs=[
            pl.BlockSpec((1, nb, rows_per, k2), lambda t, i: (t, i, 0, 0)),
            pl.BlockSpec((1, 2 * k2, hid), lambda t, i: (t, 0, 0)),
            pl.BlockSpec((1, SUBLANES, 2 * k2), lambda t, i: (t, 0, 0)),
            pl.BlockSpec((1, hid, dh), lambda t, i: (t, 0, 0)),
        ],
        out_specs=pl.BlockSpec((1, nb, rows_per, dh), lambda t, i: (t, i, 0, 0)),
        out_shape=jax.ShapeDtypeStruct((2, bg, rows_per, dh), BF16),
        compiler_params=_cparams(("parallel", "parallel")),
        name="kv_compress",
    )(xkv, w1.astype(BF16), pe8, w2.astype(BF16))


def _softmax_cols(s):
    m = jnp.max(s, axis=0, keepdims=True)
    e = jnp.exp(s - m)
    return e / jnp.sum(e, axis=0, keepdims=True)


def _attn_kernel(qt_ref, qrt_ref, gt_ref, kc_ref, vct_ref, ks_ref, vst_ref, kw_ref, vwt_ref, ovt_ref,
                 o_ref, selx_ref, *, tq, nblk, seq):
    nq = HPG * tq
    t0 = pl.program_id(2) * tq
    lane_q = lax.broadcasted_iota(I32, (1, nq), 1)
    qpos = t0 + lane_q % tq
    qt = qt_ref[0, 0, 0]
    qrt = qrt_ref[0, 0, 0]

    ncp = kc_ref.shape[2]
    sc = _dot(kc_ref[0, 0], qt)
    blk_end = lax.broadcasted_iota(I32, (ncp, 1), 0) * CMP_STRIDE + (CMP_BLOCK - 1)
    p = _softmax_cols(jnp.where(blk_end <= qpos, sc, NEG))
    p = p * (qpos >= CMP_BLOCK - 1).astype(F32)
    o_cmp = _dot(vct_ref[0, 0], p.astype(BF16))

    psum = p[:, 0:tq]
    for hh in range(1, HPG):
        psum = psum + p[:, hh * tq:(hh + 1) * tq]
    p_hi, p_lo = _split(psum)
    ovt = ovt_ref[...].astype(BF16)
    imp = _dot(ovt, p_hi) + _dot(ovt, p_lo)
    jj = lax.broadcasted_iota(I32, (nblk, 1), 0)
    qp1 = t0 + lax.broadcasted_iota(I32, (1, tq), 1)
    cur = qp1 // SEL_BLOCK
    forced = jnp.logical_or(jj == 0, jnp.logical_or(jj == cur, jj == cur - 1))
    score = jnp.where(forced, BIG, jnp.where(jj * SEL_BLOCK <= qp1, imp, NEG))
    ahead = jnp.zeros((nblk, tq), F32)
    for jp in range(nblk):
        sj = score[jp:jp + 1, :]
        first = jnp.logical_or(sj > score, jnp.logical_and(sj == score, jp < jj))
        ahead = ahead + first.astype(F32)
    sel = (ahead < float(min(N_SEL, nblk))).astype(F32)
    sel = jnp.concatenate([sel] * HPG, axis=1)
    for j in range(nblk):
        selx_ref[j * SUBLANES:(j + 1) * SUBLANES, :] = jnp.broadcast_to(sel[j:j + 1, :], (SUBLANES, nq))

    kc = SLC_KC
    bpc = kc // SEL_BLOCK
    rep = SEL_BLOCK // SUBLANES

    def chunk(ci, carry):
        m, l, acc = carry
        k0 = pl.multiple_of(ci * kc, kc)
        s = _dot(ks_ref[0, 0, pl.ds(k0, kc), :], qrt)
        selc = selx_ref[pl.ds(pl.multiple_of(ci * (bpc * SUBLANES), bpc * SUBLANES), bpc * SUBLANES), :]
        pieces = []
        for bb in range(bpc):
            pieces.extend([selc[bb * SUBLANES:(bb + 1) * SUBLANES, :]] * rep)
        selk = jnp.concatenate(pieces, axis=0)
        kpos = k0 + lax.broadcasted_iota(I32, (kc, 1), 0)
        ok = jnp.logical_and(selk > 0.5, kpos <= qpos)
        s = jnp.where(ok, s, NEG)
        m_new = jnp.maximum(m, jnp.max(s, axis=0, keepdims=True))
        alpha = jnp.exp(m - m_new)
        e = jnp.exp(s - m_new)
        l = alpha * l + jnp.sum(e, axis=0, keepdims=True)
        acc = alpha * acc + _dot(vst_ref[0, 0, :, pl.ds(k0, kc)], e.astype(BF16))
        return m_new, l, acc

    n_chunks = (t0 + tq + kc - 1) // kc
    init = (jnp.full((1, nq), NEG, F32), jnp.zeros((1, nq), F32), jnp.zeros((HEAD_DIM, nq), F32))
    _, l, acc = lax.fori_loop(0, n_chunks, chunk, init)
    o_slc = acc / l

    wk = WINDOW + tq
    k0 = pl.multiple_of(jnp.maximum(t0 - WINDOW, 0), tq)
    s = _dot(kw_ref[0, 0, pl.ds(k0, wk), :], qrt)
    kpos = k0 + lax.broadcasted_iota(I32, (wk, 1), 0)
    ok = jnp.logical_and(kpos <= qpos, kpos > qpos - WINDOW)
    pw = _softmax_cols(jnp.where(ok, s, NEG))
    o_win = _dot(vwt_ref[0, 0, :, pl.ds(k0, wk)], pw.astype(BF16))

    gt = gt_ref[0, 0, 0]
    o = gt[0:1, :] * o_cmp + gt[1:2, :] * o_slc + gt[2:3, :] * o_win
    o_ref[0, 0, 0] = o.astype(BF16)


def _nsa_attention(qt, qrt, gt, kc, vct, ks, vst, kw, vwt, ovt, seq):
    bsz, ng, nt, dh, nq = qt.shape
    tq = nq // HPG
    nblk = seq // SEL_BLOCK
    ncp = kc.shape[2]
    qspec = pl.BlockSpec((1, 1, 1, dh, nq), lambda b, g, i: (b, g, i, 0, 0))
    kern = functools.partial(_attn_kernel, tq=tq, nblk=nblk, seq=seq)
    return pl.pallas_call(
        kern,
        grid=(bsz, ng, nt),
        in_specs=[
            qspec, qspec,
            pl.BlockSpec((1, 1, 1, 3, nq), lambda b, g, i: (b, g, i, 0, 0)),
            pl.BlockSpec((1, 1, ncp, dh), lambda b, g, i: (b, g, 0, 0)),
            pl.BlockSpec((1, 1, dh, ncp), lambda b, g, i: (b, g, 0, 0)),
            pl.BlockSpec((1, 1, seq, dh), lambda b, g, i: (b, g, 0, 0)),
            pl.BlockSpec((1, 1, dh, seq), lambda b, g, i: (b, g, 0, 0)),
            pl.BlockSpec((1, 1, seq, dh), lambda b, g, i: (b, g, 0, 0)),
            pl.BlockSpec((1, 1, dh, seq), lambda b, g, i: (b, g, 0, 0)),
            pl.BlockSpec((nblk, ncp), lambda b, g, i: (0, 0)),
        ],
        out_specs=qspec,
        out_shape=jax.ShapeDtypeStruct((bsz, ng, nt, dh, nq), BF16),
        scratch_shapes=[pltpu.VMEM((nblk * SUBLANES, nq), F32)],
        compiler_params=_cparams(("parallel", "parallel", "arbitrary")),
        name="nsa_attention",
    )(qt, qrt, gt, kc, vct, ks, vst, kw, vwt, ovt)


def _overlap_t(seq):
    ncp = seq // CMP_STRIDE
    nblk = seq // SEL_BLOCK
    c_start = jnp.arange(ncp) * CMP_STRIDE
    j_start = jnp.arange(nblk) * SEL_BLOCK
    ov = (c_start[None, :] < j_start[:, None] + SEL_BLOCK) & (c_start[None, :] + CMP_BLOCK > j_start[:, None])
    return ov.astype(F32)


def _shared_kv(h2, w_kv, cmp_pos, cmp_w1, cmp_w2, tabs, bsz, seq):
    kv = _kv_proj(h2, w_kv, tabs, seq).reshape(bsz, seq, 6, N_KV, HEAD_DIM)
    rows_per = seq // CMP_STRIDE
    xkv = kv[:, :, 0:2].reshape(bsz, rows_per, CMP_STRIDE, 2, N_KV, HEAD_DIM)
    xkv = xkv.transpose(3, 0, 4, 1, 2, 5).reshape(2, bsz * N_KV, rows_per, CMP_STRIDE * HEAD_DIM)
    cmp = _compress(xkv, cmp_w1, cmp_pos, cmp_w2).reshape(2, bsz, N_KV, rows_per, HEAD_DIM)
    kc = cmp[0]
    vct = cmp[1].transpose(0, 1, 3, 2)
    ks = kv[:, :, 2].transpose(0, 2, 1, 3)
    vst = kv[:, :, 3].transpose(0, 2, 3, 1)
    kw = kv[:, :, 4].transpose(0, 2, 1, 3)
    vwt = kv[:, :, 5].transpose(0, 2, 3, 1)
    return kc, vct, ks, vst, kw, vwt


def _nsa_layer(h2, w_qg, w_o, shared, tabs, ovt, g, b, bsz, seq):
    tq = min(ATT_TQ, seq)
    nt = seq // tq
    q, qr, gates = _qg_proj(h2, w_qg, tabs, seq)

    def to_t(a):
        a = a.reshape(bsz, nt, tq, N_KV, HPG, HEAD_DIM)
        return a.transpose(0, 3, 1, 5, 4, 2).reshape(bsz, N_KV, nt, HEAD_DIM, HPG * tq)

    gt = gates[:, :N_HEADS * 3].reshape(bsz, nt, tq, N_KV, HPG, 3)
    gt = gt.transpose(0, 3, 1, 5, 4, 2).reshape(bsz, N_KV, nt, 3, HPG * tq)
    ot = _nsa_attention(to_t(q), to_t(qr), gt, *shared, ovt, seq)
    o = ot.reshape(bsz, N_KV, nt, HEAD_DIM, HPG, tq).transpose(0, 2, 5, 1, 4, 3)
    o = o.reshape(bsz * seq, N_HEADS * HEAD_DIM)
    return _mm_postnorm(o, w_o, h2, g, b)


def kernel(x, ln_g, ln_b, lru_w_in, lru_conv_w, lru_conv_b, lru_w_a, lru_b_a, lru_w_i, lru_b_i, lru_lambda,
           lru_w_out, nsa_w_kv, nsa_cmp_pos, nsa_cmp_w1, nsa_cmp_w2, nsa_w_qg, nsa_w_o, ffn_w_gu, ffn_w_down,
           moe_w_router, moe_w_gu, moe_w_down):
    bsz, seq, d = x.shape
    tabs = _rope_tables(seq)
    ovt = _overlap_t(seq)
    h = x
    shared = None
    for l in range(DEPTH):
        if l < N_A_LAYERS:
            h = _lru_layer(h, lru_w_in[l], lru_conv_w[l], lru_conv_b[l], lru_w_a[l], lru_b_a[l].reshape(-1),
                           lru_w_i[l], lru_b_i[l].reshape(-1), lru_lambda[l], lru_w_out[l],
                           ln_g[l, 0], ln_b[l, 0])
            h2 = h.reshape(bsz * seq, d)
        else:
            lb = l - N_A_LAYERS
            h2 = _nsa_layer(h2, nsa_w_qg[lb], nsa_w_o[lb], shared, tabs, ovt, ln_g[l, 0], ln_b[l, 0], bsz, seq)
        if l % 2 == 0:
            h2 = _dense_ffn(h2, ffn_w_gu[l // 2], ffn_w_down[l // 2], ln_g[l, 1], ln_b[l, 1])
        else:
            h2 = _moe(h2, moe_w_router[l // 2], moe_w_gu[l // 2], moe_w_down[l // 2], ln_g[l, 1], ln_b[l, 1])
        h = h2.reshape(bsz, seq, d)
        if l == N_A_LAYERS - 1:
            shared = _shared_kv(h2, nsa_w_kv, nsa_cmp_pos, nsa_cmp_w1, nsa_cmp_w2, tabs, bsz, seq)
    return h
```

```python
import functools

import jax
import jax.numpy as jnp
from jax import lax
from jax.experimental import pallas as pl
from jax.experimental.pallas import tpu as pltpu

F32 = jnp.float32
BF16 = jnp.bfloat16
I32 = jnp.int32

DEPTH = 4
N_A_LAYERS = DEPTH // 2
LRU_BLOCKS = 8
CONV_WIDTH = 4
LRU_C = 8.0
N_HEADS = 16
N_KV = 4
HPG = N_HEADS // N_KV
HEAD_DIM = 64
ROT_DIM = HEAD_DIM // 4
ROPE_THETA = 500000.0
CMP_BLOCK = 32
CMP_STRIDE = 16
SEL_BLOCK = 64
N_SEL = 8
WINDOW = 256
N_EXPERTS = 8
DN_ALPHA = (2.0 * DEPTH) ** 0.25
LN_EPS = 1e-5
NEG = -1e30
BIG = 1e30
LOG2E = 1.4426950408889634

LANES = 128
SUBLANES = 8
VMEM_LIMIT = 52 * 1024 * 1024

LRU_TS = 256
FFN_TM = 1024
ROUTE_TM = 512
ROW_TM = 256
PROJ_TM = 512
ATT_TQ = 256
SLC_KC = 256
DMA_UNROLL = 8


def _cparams(sem):
    return pltpu.CompilerParams(dimension_semantics=sem, vmem_limit_bytes=VMEM_LIMIT)


def _dot(a, b):
    return jnp.dot(a, b, preferred_element_type=F32)


def _split(a):
    hi = a.astype(BF16)
    lo = (a - hi.astype(F32)).astype(BF16)
    return hi, lo


def _dot_f32(a, b):
    a_hi, a_lo = _split(a)
    b_hi, b_lo = _split(b)
    return _dot(a_hi, b_hi) + (_dot(a_hi, b_lo) + _dot(a_lo, b_hi))


def _postnorm(res, y, g, b):
    z = DN_ALPHA * res + y
    mu = jnp.mean(z, axis=-1, keepdims=True)
    zc = z - mu
    var = jnp.mean(zc * zc, axis=-1, keepdims=True)
    return zc * lax.rsqrt(var + LN_EPS) * g + b


def _lru_kernel(x_ref, win_ref, cw_ref, cb_ref, wai_ref, ba_ref, bi_ref, lam_ref, wout_ref,
                g_ref, b_ref, o_ref, hcar_ref, xprev_ref, hg_ref, *, ts, width, nblk):
    bw = width // nblk

    @pl.when(pl.program_id(1) == 0)
    def _():
        hcar_ref[...] = jnp.zeros_like(hcar_ref)
        xprev_ref[...] = jnp.zeros_like(xprev_ref)

    x = x_ref[0]
    u = _dot(x.astype(BF16), win_ref[...])
    xc = u[:, width:]
    xcat = jnp.concatenate([xprev_ref[...], xc], axis=0)
    xr = cb_ref[...] + cw_ref[CONV_WIDTH - 1:CONV_WIDTH, :] * xc
    for s in range(1, CONV_WIDTH):
        xr = xr + cw_ref[CONV_WIDTH - 1 - s:CONV_WIDTH - s, :] * xcat[SUBLANES - s:SUBLANES - s + ts]
    xprev_ref[...] = xc[ts - SUBLANES:ts]

    row = lax.broadcasted_iota(I32, (ts // SUBLANES, SUBLANES, bw), 1)
    for n in range(nblk):
        sl = slice(n * bw, (n + 1) * bw)
        xb = xr[:, sl]
        ri = _dot(xb.astype(BF16), wai_ref[n])
        r = jax.nn.sigmoid(ri[:, :bw] + ba_ref[:, sl])
        i = jax.nn.sigmoid(ri[:, bw:] + bi_ref[:, sl])
        lam = lam_ref[:, sl]
        sp = jnp.maximum(-lam, 0.0) + jnp.log(1.0 + jnp.exp(-jnp.abs(lam)))
        log_a = (-LRU_C * r) * sp
        a = jnp.exp(log_a)
        th = jnp.tanh(log_a)
        h = jnp.sqrt(-2.0 * th / (1.0 - th)) * (i * xb)
        a = a.reshape(ts // SUBLANES, SUBLANES, bw)
        h = h.reshape(ts // SUBLANES, SUBLANES, bw)
        d = 1
        while d < SUBLANES:
            keep = row >= d
            a_sh = jnp.where(keep, pltpu.roll(a, d, 1), 1.0)
            h_sh = jnp.where(keep, pltpu.roll(h, d, 1), 0.0)
            h = a * h_sh + h
            a = a * a_sh
            d *= 2
        carry = hcar_ref[:, sl]
        groups = []
        for v in range(ts // SUBLANES):
            hv = h[v] + a[v] * carry
            carry = hv[SUBLANES - 1:SUBLANES, :]
            groups.append(hv)
        h = jnp.concatenate(groups, axis=0)
        hcar_ref[:, sl] = carry
        gate = jax.nn.gelu(u[:, sl])
        hg_ref[:, sl] = (h * gate).astype(BF16)

    y = _dot(hg_ref[...], wout_ref[...])
    o_ref[0] = _postnorm(x, y, g_ref[...], b_ref[...])


def _lru_layer(x, w_in, conv_w, conv_b, w_a, b_a, w_i, b_i, lam, w_out, g, b):
    bsz, seq, d = x.shape
    width = w_out.shape[0]
    nblk = w_a.shape[0]
    ts = min(LRU_TS, seq)
    wai = jnp.concatenate([w_a, w_i], axis=-1).astype(BF16)
    row2 = lambda v: v.reshape(1, -1)
    const2 = lambda bi, ti: (0, 0)
    kern = functools.partial(_lru_kernel, ts=ts, width=width, nblk=nblk)
    return pl.pallas_call(
        kern,
        grid=(bsz, seq // ts),
        in_specs=[
            pl.BlockSpec((1, ts, d), lambda bi, ti: (bi, ti, 0)),
            pl.BlockSpec((d, 2 * width), const2),
            pl.BlockSpec((CONV_WIDTH, width), const2),
            pl.BlockSpec((1, width), const2),
            pl.BlockSpec(wai.shape, lambda bi, ti: (0, 0, 0)),
            pl.BlockSpec((1, width), const2),
            pl.BlockSpec((1, width), const2),
            pl.BlockSpec((1, width), const2),
            pl.BlockSpec((width, d), const2),
            pl.BlockSpec((1, d), const2),
            pl.BlockSpec((1, d), const2),
        ],
        out_specs=pl.BlockSpec((1, ts, d), lambda bi, ti: (bi, ti, 0)),
        out_shape=jax.ShapeDtypeStruct((bsz, seq, d), F32),
        scratch_shapes=[
            pltpu.VMEM((1, width), F32),
            pltpu.VMEM((SUBLANES, width), F32),
            pltpu.VMEM((ts, width), BF16),
        ],
        compiler_params=_cparams(("parallel", "arbitrary")),
        name="lru_layer",
    )(x, w_in.astype(BF16), conv_w, row2(conv_b), wai, row2(b_a), row2(b_i), row2(lam),
      w_out.astype(BF16), row2(g), row2(b))


def _ffn_kernel(te_ref, nu_ref, x_ref, wg_ref, wu_ref, wd_ref, g_ref, b_ref, o_ref, xb_ref, acc_ref,
                *, nf, postnorm):
    i = pl.program_id(0)
    f = pl.program_id(1)
    used = i < nu_ref[0]

    @pl.when(used)
    def _():
        @pl.when(f == 0)
        def _():
            xb_ref[...] = x_ref[...].astype(BF16)
            acc_ref[...] = jnp.zeros_like(acc_ref)

        xb = xb_ref[...]
        gp = _dot(xb, wg_ref[0].astype(BF16))
        up = _dot(xb, wu_ref[0].astype(BF16))
        act = (gp * jax.nn.sigmoid(gp) * up).astype(BF16)
        acc_ref[...] += _dot(act, wd_ref[0].astype(BF16))

        @pl.when(f == nf - 1)
        def _():
            if postnorm:
                o_ref[...] = _postnorm(x_ref[...], acc_ref[...], g_ref[...], b_ref[...])
            else:
                o_ref[...] = acc_ref[...]

    @pl.when(jnp.logical_and(jnp.logical_not(used), f == nf - 1))
    def _():
        o_ref[...] = jnp.zeros_like(o_ref)


def _ffn(x, w_gu, w_down, tile_expert, n_used, g, b, *, postnorm, tf):
    rows, d = x.shape
    ff = w_down.shape[1]
    tm = min(FFN_TM, rows)
    nf = ff // tf
    nt = rows // tm

    def fidx(i, f, nu):
        return jnp.where(i < nu[0], f, nf - 1)

    kern = functools.partial(_ffn_kernel, nf=nf, postnorm=postnorm)
    grid_spec = pltpu.PrefetchScalarGridSpec(
        num_scalar_prefetch=2,
        grid=(nt, nf),
        in_specs=[
            pl.BlockSpec((tm, d), lambda i, f, te, nu: (i, 0)),
            pl.BlockSpec((1, d, tf), lambda i, f, te, nu: (te[i], 0, fidx(i, f, nu))),
            pl.BlockSpec((1, d, tf), lambda i, f, te, nu: (te[i], 0, fidx(i, f, nu) + nf)),
            pl.BlockSpec((1, tf, d), lambda i, f, te, nu: (te[i], fidx(i, f, nu), 0)),
            pl.BlockSpec((1, d), lambda i, f, te, nu: (0, 0)),
            pl.BlockSpec((1, d), lambda i, f, te, nu: (0, 0)),
        ],
        out_specs=pl.BlockSpec((tm, d), lambda i, f, te, nu: (i, 0)),
        scratch_shapes=[pltpu.VMEM((tm, d), BF16), pltpu.VMEM((tm, d), F32)],
    )
    return pl.pallas_call(
        kern,
        grid_spec=grid_spec,
        out_shape=jax.ShapeDtypeStruct((rows, d), F32),
        compiler_params=_cparams(("parallel", "arbitrary")),
        name="ffn_postnorm" if postnorm else "ffn_experts",
    )(tile_expert, n_used, x, w_gu, w_gu, w_down, g.reshape(1, -1), b.reshape(1, -1))


def _dense_ffn(h2, w_gu, w_down, g, b):
    rows = h2.shape[0]
    nt = rows // min(FFN_TM, rows)
    return _ffn(h2, w_gu[None], w_down[None],
                jnp.zeros((nt,), I32), jnp.full((1,), nt, I32), g, b, postnorm=True, tf=256)


def _router_kernel(h_ref, wr_ref, mi_ref, mw_ref, cnt_ref, run_ref, *, tm, n_exp):
    @pl.when(pl.program_id(0) == 0)
    def _():
        run_ref[...] = jnp.zeros_like(run_ref)

    logits = _dot_f32(h_ref[...], wr_ref[...])
    lane = lax.broadcasted_iota(I32, (tm, LANES), 1)
    lg = jnp.where(lane < n_exp, logits, -jnp.inf)
    m0 = jnp.max(lg, axis=1, keepdims=True)
    i0 = jnp.min(jnp.where(lg == m0, lane, LANES), axis=1, keepdims=True)
    lg1 = jnp.where(lane == i0, -jnp.inf, lg)
    m1 = jnp.max(lg1, axis=1, keepdims=True)
    i1 = jnp.min(jnp.where(lg1 == m1, lane, LANES), axis=1, keepdims=True)
    e1 = jnp.exp(m1 - m0)
    w0 = 1.0 / (1.0 + e1)
    w1 = e1 / (1.0 + e1)

    oh0 = lane == i0
    oh1 = lane == i1
    oh = jnp.logical_or(oh0, oh1).astype(F32)
    tri = (lax.broadcasted_iota(I32, (tm, tm), 0) > lax.broadcasted_iota(I32, (tm, tm), 1)).astype(BF16)
    before = _dot(tri, oh.astype(BF16)) + run_ref[0:1, :]
    r0 = jnp.sum(jnp.where(oh0, before, 0.0), axis=1, keepdims=True).astype(I32)
    r1 = jnp.sum(jnp.where(oh1, before, 0.0), axis=1, keepdims=True).astype(I32)
    run_ref[0:1, :] = run_ref[0:1, :] + jnp.sum(oh, axis=0, keepdims=True)

    mi_ref[...] = jnp.where(lane == 0, i0, jnp.where(lane == 1, i1, jnp.where(lane == 2, r0, r1)))
    mw_ref[...] = jnp.where(lane == 0, w0, w1)
    cnt_ref[...] = run_ref[...]


def _router(h2, w_router):
    rows, d = h2.shape
    n_exp = w_router.shape[1]
    tm = min(ROUTE_TM, rows)
    wr = jnp.pad(w_router, ((0, 0), (0, LANES - n_exp)))
    kern = functools.partial(_router_kernel, tm=tm, n_exp=n_exp)
    return pl.pallas_call(
        kern,
        grid=(rows // tm,),
        in_specs=[pl.BlockSpec((tm, d), lambda i: (i, 0)), pl.BlockSpec((d, LANES), lambda i: (0, 0))],
        out_specs=[
            pl.BlockSpec((tm, LANES), lambda i: (i, 0)),
            pl.BlockSpec((tm, LANES), lambda i: (i, 0)),
            pl.BlockSpec((SUBLANES, LANES), lambda i: (0, 0)),
        ],
        out_shape=[
            jax.ShapeDtypeStruct((rows, LANES), I32),
            jax.ShapeDtypeStruct((rows, LANES), F32),
            jax.ShapeDtypeStruct((SUBLANES, LANES), F32),
        ],
        scratch_shapes=[pltpu.VMEM((SUBLANES, LANES), F32)],
        compiler_params=_cparams(("arbitrary",)),
        name="moe_router",
    )(h2, wr)


def _row_copy(src, dst, sem):
    return pltpu.make_async_copy(src, dst, sem)


def _scatter_kernel(pos_ref, h_ref, zin_ref, xs_ref, sem, *, tm):
    del zin_ref

    def start(r, c):
        row = h_ref.at[pl.ds(r, 1)]
        _row_copy(row, xs_ref.at[pl.ds(pos_ref[0, 0, 2 * r], 1)], sem).start()
        _row_copy(row, xs_ref.at[pl.ds(pos_ref[0, 0, 2 * r + 1], 1)], sem).start()
        return c

    lax.fori_loop(0, tm, start, 0, unroll=DMA_UNROLL)

    def wait(r, c):
        row = h_ref.at[pl.ds(r, 1)]
        _row_copy(row, xs_ref.at[pl.ds(pos_ref[0, 0, 2 * r], 1)], sem).wait()
        _row_copy(row, xs_ref.at[pl.ds(pos_ref[0, 0, 2 * r + 1], 1)], sem).wait()
        return c

    lax.fori_loop(0, tm, wait, 0, unroll=DMA_UNROLL)


def _scatter_rows(h2, pos, n_sorted):
    rows, d = h2.shape
    tm = min(ROW_TM, rows)
    nt = rows // tm
    kern = functools.partial(_scatter_kernel, tm=tm)
    return pl.pallas_call(
        kern,
        grid=(nt,),
        in_specs=[
            pl.BlockSpec((1, 1, 2 * tm), lambda i: (i, 0, 0), memory_space=pltpu.SMEM),
            pl.BlockSpec((tm, d), lambda i: (i, 0)),
            pl.BlockSpec(memory_space=pl.ANY),
        ],
        out_specs=pl.BlockSpec(memory_space=pl.ANY),
        out_shape=jax.ShapeDtypeStruct((n_sorted, d), F32),
        scratch_shapes=[pltpu.SemaphoreType.DMA(())],
        input_output_aliases={2: 0},
        compiler_params=_cparams(("arbitrary",)),
        name="moe_scatter",
    )(pos.reshape(nt, 1, 2 * tm), h2, jnp.zeros((n_sorted, d), F32))


def _combine_kernel(pos_ref, mw_ref, h_ref, ys_ref, g_ref, b_ref, o_ref, y0_ref, y1_ref, sem, *, tm):
    def start(r, c):
        _row_copy(ys_ref.at[pl.ds(pos_ref[0, 0, 2 * r], 1)], y0_ref.at[pl.ds(r, 1)], sem).start()
        _row_copy(ys_ref.at[pl.ds(pos_ref[0, 0, 2 * r + 1], 1)], y1_ref.at[pl.ds(r, 1)], sem).start()
        return c

    lax.fori_loop(0, tm, start, 0, unroll=DMA_UNROLL)

    def wait(r, c):
        _row_copy(ys_ref.at[pl.ds(pos_ref[0, 0, 2 * r], 1)], y0_ref.at[pl.ds(r, 1)], sem).wait()
        _row_copy(ys_ref.at[pl.ds(pos_ref[0, 0, 2 * r + 1], 1)], y1_ref.at[pl.ds(r, 1)], sem).wait()
        return c

    lax.fori_loop(0, tm, wait, 0, unroll=DMA_UNROLL)
    w = mw_ref[...]
    y = w[:, 0:1] * y0_ref[...] + w[:, 1:2] * y1_ref[...]
    o_ref[...] = _postnorm(h_ref[...], y, g_ref[...], b_ref[...])


def _combine_rows(h2, ys, pos, mw, g, b):
    rows, d = h2.shape
    tm = min(ROW_TM, rows)
    nt = rows // tm
    kern = functools.partial(_combine_kernel, tm=tm)
    return pl.pallas_call(
        kern,
        grid=(nt,),
        in_specs=[
            pl.BlockSpec((1, 1, 2 * tm), lambda i: (i, 0, 0), memory_space=pltpu.SMEM),
            pl.BlockSpec((tm, LANES), lambda i: (i, 0)),
            pl.BlockSpec((tm, d), lambda i: (i, 0)),
            pl.BlockSpec(memory_space=pl.ANY),
            pl.BlockSpec((1, d), lambda i: (0, 0)),
            pl.BlockSpec((1, d), lambda i: (0, 0)),
        ],
        out_specs=pl.BlockSpec((tm, d), lambda i: (i, 0)),
        out_shape=jax.ShapeDtypeStruct((rows, d), F32),
        scratch_shapes=[pltpu.VMEM((tm, d), F32), pltpu.VMEM((tm, d), F32), pltpu.SemaphoreType.DMA(())],
        compiler_params=_cparams(("arbitrary",)),
        name="moe_combine",
    )(pos.reshape(nt, 1, 2 * tm), mw, h2, ys, g.reshape(1, -1), b.reshape(1, -1))


def _moe(h2, w_router, w_gu, w_down, g, b):
    rows, d = h2.shape
    n_exp = w_router.shape[1]
    tm = min(FFN_TM, rows)
    mi, mw, cnt = _router(h2, w_router)
    counts = cnt[0, :n_exp].astype(I32)
    tiles_per = (counts + tm - 1) // tm
    tile_end = jnp.cumsum(tiles_per)
    row_start = (tile_end - tiles_per) * tm
    e01 = mi[:, 0:2]
    onehot = e01[:, :, None] == jnp.arange(n_exp, dtype=I32)[None, None, :]
    pos = jnp.sum(jnp.where(onehot, row_start[None, None, :], 0), axis=-1) + mi[:, 2:4]
    n_tiles = (2 * rows) // tm + n_exp
    n_used = tile_end[-1]
    tid = jnp.minimum(jnp.arange(n_tiles, dtype=I32), n_used - 1)
    tile_expert = jnp.sum((tid[:, None] >= tile_end[None, :]).astype(I32), axis=1)
    tile_expert = jnp.minimum(tile_expert, n_exp - 1)

    xs = _scatter_rows(h2, pos, n_tiles * tm)
    ys = _ffn(xs, w_gu, w_down, tile_expert, n_used.reshape(1), g, b, postnorm=False, tf=512)
    return _combine_rows(h2, ys, pos, mw, g, b)


def _rope_cols(x, c, sa, sb):
    half = ROT_DIM // 2
    return x * c + pltpu.roll(x, LANES - half, 1) * sa + pltpu.roll(x, half, 1) * sb


def _rope_tables(seq):
    half = ROT_DIM // 2
    inv = ROPE_THETA ** (-jnp.arange(half, dtype=F32) / half)
    ang = jnp.arange(seq, dtype=F32)[:, None] * inv[None]
    cos, sin = jnp.cos(ang), jnp.sin(ang)
    ones = jnp.ones((seq, HEAD_DIM - ROT_DIM), F32)
    zeros = jnp.zeros((seq, HEAD_DIM - ROT_DIM), F32)
    zh = jnp.zeros((seq, half), F32)
    c = jnp.concatenate([cos, cos, ones], axis=1)
    sa = jnp.concatenate([-sin, zh, zeros], axis=1)
    sb = jnp.concatenate([zh, sin, zeros], axis=1)
    rep = LANES // HEAD_DIM
    return tuple(jnp.tile(t, (1, rep)) for t in (c, sa, sb))


def _kv_kernel(h_ref, w_ref, c_ref, sa_ref, sb_ref, raw_ref, ks_ref, vst_ref, kw_ref, vwt_ref, *, gw):
    res = _dot(h_ref[0].astype(BF16), w_ref[...])
    c, sa, sb = c_ref[...], sa_ref[...], sb_ref[...]
    ng = gw // HEAD_DIM

    def part(p):
        return res[:, p * gw:(p + 1) * gw]

    def roped(p):
        x = part(p)
        return jnp.concatenate(
            [_rope_cols(x[:, j * LANES:(j + 1) * LANES], c, sa, sb) for j in range(gw // LANES)], axis=1)

    for t in range(2):
        x = part(t)
        for g in range(ng):
            raw_ref[t, 0, g] = x[:, g * HEAD_DIM:(g + 1) * HEAD_DIM]
    for p, k_ref, vt_ref in ((2, ks_ref, vst_ref), (4, kw_ref, vwt_ref)):
        k = roped(p)
        vt = part(p + 1).T
        for g in range(ng):
            k_ref[0, g] = k[:, g * HEAD_DIM:(g + 1) * HEAD_DIM].astype(BF16)
            vt_ref[0, g] = vt[g * HEAD_DIM:(g + 1) * HEAD_DIM, :].astype(BF16)


def _kv_proj(h3, w_kv, tabs):
    bsz, seq, d = h3.shape
    n = w_kv.shape[1]
    gw = n // 6
    ng = gw // HEAD_DIM
    tm = min(PROJ_TM, seq)
    tab_spec = pl.BlockSpec((tm, LANES), lambda b, i: (i, 0))
    k_spec = pl.BlockSpec((1, ng, tm, HEAD_DIM), lambda b, i: (b, 0, i, 0))
    vt_spec = pl.BlockSpec((1, ng, HEAD_DIM, tm), lambda b, i: (b, 0, 0, i))
    k_shape = jax.ShapeDtypeStruct((bsz, ng, seq, HEAD_DIM), BF16)
    vt_shape = jax.ShapeDtypeStruct((bsz, ng, HEAD_DIM, seq), BF16)
    kern = functools.partial(_kv_kernel, gw=gw)
    return pl.pallas_call(
        kern,
        grid=(bsz, seq // tm),
        in_specs=[pl.BlockSpec((1, tm, d), lambda b, i: (b, i, 0)), pl.BlockSpec((d, n), lambda b, i: (0, 0)),
                  tab_spec, tab_spec, tab_spec],
        out_specs=[pl.BlockSpec((2, 1, ng, tm, HEAD_DIM), lambda b, i: (0, b, 0, i, 0)),
                   k_spec, vt_spec, k_spec, vt_spec],
        out_shape=[jax.ShapeDtypeStruct((2, bsz, ng, seq, HEAD_DIM), F32), k_shape, vt_shape, k_shape, vt_shape],
        compiler_params=_cparams(("parallel", "parallel")),
        name="kv_proj",
    )(h3, w_kv.astype(BF16), *tabs)


def _qg_kernel(h_ref, w_ref, c_ref, sa_ref, sb_ref, q_ref, qr_ref, gt_ref, *, nq):
    res = _dot(h_ref[0].astype(BF16), w_ref[...])
    c, sa, sb = c_ref[...], sa_ref[...], sb_ref[...]
    scale = HEAD_DIM ** -0.5 * LOG2E
    for j in range(nq // LANES):
        x = res[:, j * LANES:(j + 1) * LANES] * scale
        q_ref[0, :, j * LANES:(j + 1) * LANES] = x.astype(BF16)
        qr_ref[0, :, j * LANES:(j + 1) * LANES] = _rope_cols(x, c, sa, sb).astype(BF16)
    gt_ref[0] = jax.nn.sigmoid(res[:, nq:])


def _qg_proj(h3, w_qg, tabs):
    bsz, seq, d = h3.shape
    nq = N_HEADS * HEAD_DIM
    ngate = HPG * 3
    tm = min(PROJ_TM, seq)
    wg = w_qg[:, nq:].reshape(d, N_KV, ngate)
    wg = jnp.pad(wg, ((0, 0), (0, 0), (0, LANES - ngate))).reshape(d, N_KV * LANES)
    w = jnp.concatenate([w_qg[:, :nq], wg], axis=1).astype(BF16)
    n = w.shape[1]
    tab_spec = pl.BlockSpec((tm, LANES), lambda b, i: (i, 0))
    kern = functools.partial(_qg_kernel, nq=nq)
    return pl.pallas_call(
        kern,
        grid=(bsz, seq // tm),
        in_specs=[pl.BlockSpec((1, tm, d), lambda b, i: (b, i, 0)), pl.BlockSpec((d, n), lambda b, i: (0, 0)),
                  tab_spec, tab_spec, tab_spec],
        out_specs=[pl.BlockSpec((1, tm, nq), lambda b, i: (b, i, 0)), pl.BlockSpec((1, tm, nq), lambda b, i: (b, i, 0)),
                   pl.BlockSpec((1, tm, N_KV * LANES), lambda b, i: (b, i, 0))],
        out_shape=[jax.ShapeDtypeStruct((bsz, seq, nq), BF16), jax.ShapeDtypeStruct((bsz, seq, nq), BF16),
                   jax.ShapeDtypeStruct((bsz, seq, N_KV * LANES), F32)],
        compiler_params=_cparams(("parallel", "parallel")),
        name="qg_proj",
    )(h3, w, *tabs)


def _mm_ln_kernel(a_ref, w_ref, res_ref, g_ref, b_ref, o_ref):
    y = _dot(a_ref[...], w_ref[...])
    o_ref[...] = _postnorm(res_ref[...], y, g_ref[...], b_ref[...])


def _mm_postnorm(a, w, res, g, b):
    rows, k = a.shape
    d = w.shape[1]
    tm = min(PROJ_TM, rows)
    return pl.pallas_call(
        _mm_ln_kernel,
        grid=(rows // tm,),
        in_specs=[pl.BlockSpec((tm, k), lambda i: (i, 0)), pl.BlockSpec((k, d), lambda i: (0, 0)),
                  pl.BlockSpec((tm, d), lambda i: (i, 0)),
                  pl.BlockSpec((1, d), lambda i: (0, 0)), pl.BlockSpec((1, d), lambda i: (0, 0))],
        out_specs=pl.BlockSpec((tm, d), lambda i: (i, 0)),
        out_shape=jax.ShapeDtypeStruct((rows, d), F32),
        compiler_params=_cparams(("parallel",)),
        name="out_proj_postnorm",
    )(a, w.astype(BF16), res, g.reshape(1, -1), b.reshape(1, -1))


def _compress_kernel(raw_ref, w1_ref, pe_ref, w2_ref, o_ref, ot_ref, *, rows_per):
    half = w1_ref.shape[1] // 2
    pieces = [raw_ref[0, 0, 0, pl.ds(l, rows_per, stride=CMP_STRIDE), :] for l in range(CMP_STRIDE)]
    x = jnp.concatenate(pieces, axis=1).astype(BF16)
    first = _dot(x, w1_ref[0, :half, :])
    second = _dot(x, w1_ref[0, half:, :])
    bias = _dot(pe_ref[0].astype(BF16), w1_ref[0])[0:1, :]
    hid = first + pltpu.roll(second, rows_per - 1, 0) + bias
    out = _dot(jax.nn.gelu(hid).astype(BF16), w2_ref[0])
    o_ref[0, 0, 0] = out.astype(BF16)
    wide = jnp.concatenate([out, jnp.zeros_like(out)], axis=1)
    ot_ref[0, 0, 0] = wide.T[:out.shape[1], :].astype(BF16)


def _compress(raw, w1, pe, w2):
    _, bsz, ng, seq, dh = raw.shape
    rows_per = seq // CMP_STRIDE
    k2 = w1.shape[1]
    hid = w1.shape[2]
    pe8 = jnp.broadcast_to(pe.reshape(2, 1, k2), (2, SUBLANES, k2))
    kern = functools.partial(_compress_kernel, rows_per=rows_per)
    return pl.pallas_call(
        kern,
        grid=(2, bsz, ng),
        in_specs=[
            pl.BlockSpec((1, 1, 1, seq, dh), lambda t, b, g: (t, b, g, 0, 0)),
            pl.BlockSpec((1, k2, hid), lambda t, b, g: (t, 0, 0)),
            pl.BlockSpec((1, SUBLANES, k2), lambda t, b, g: (t, 0, 0)),
            pl.BlockSpec((1, hid, dh), lambda t, b, g: (t, 0, 0)),
        ],
        out_specs=[pl.BlockSpec((1, 1, 1, rows_per, dh), lambda t, b, g: (t, b, g, 0, 0)),
                   pl.BlockSpec((1, 1, 1, dh, rows_per), lambda t, b, g: (t, b, g, 0, 0))],
        out_shape=[jax.ShapeDtypeStruct((2, bsz, ng, rows_per, dh), BF16),
                   jax.ShapeDtypeStruct((2, bsz, ng, dh, rows_per), BF16)],
        compiler_params=_cparams(("parallel", "parallel", "parallel")),
        name="kv_compress",
    )(raw, w1.astype(BF16), pe8, w2.astype(BF16))


def _softmax2_cols(s):
    m = jnp.max(s, axis=0, keepdims=True)
    e = jnp.exp2(s - m)
    return e * (1.0 / jnp.sum(e, axis=0, keepdims=True))


def _heads_t(x):
    xt = x.astype(F32).T
    return jnp.concatenate([xt[h * HEAD_DIM:(h + 1) * HEAD_DIM, :] for h in range(HPG)], axis=1)


def _attn_kernel(q_ref, qr_ref, gt_ref, kc_ref, vct_ref, ks_ref, vst_ref, kw_ref, vwt_ref, ovt_ref,
                 diagb_ref, winb_ref, o_ref, selb_ref, sbuf_ref, ebuf_ref, *, tq, nblk):
    nq = HPG * tq
    it = pl.program_id(2)
    t0 = it * tq
    lane_q = lax.broadcasted_iota(I32, (1, nq), 1)
    qpos = t0 + lane_q % tq
    qt = _heads_t(q_ref[0]).astype(BF16)
    qrt = _heads_t(qr_ref[0]).astype(BF16)

    ncp = kc_ref.shape[3]
    sc = _dot(kc_ref[0, 0, 0], qt)
    blk_end = lax.broadcasted_iota(I32, (ncp, 1), 0) * CMP_STRIDE + (CMP_BLOCK - 1)
    p = _softmax2_cols(jnp.where(blk_end <= qpos, sc, NEG))
    p = p * (qpos >= CMP_BLOCK - 1).astype(F32)
    o_cmp = _dot(vct_ref[0, 0, 0], p.astype(BF16))

    psum = p[:, 0:tq]
    for hh in range(1, HPG):
        psum = psum + p[:, hh * tq:(hh + 1) * tq]
    p_hi, p_lo = _split(psum)
    ovt = ovt_ref[...].astype(BF16)
    imp = _dot(ovt, p_hi) + _dot(ovt, p_lo)
    jj = lax.broadcasted_iota(I32, (nblk, 1), 0)
    qp1 = t0 + lax.broadcasted_iota(I32, (1, tq), 1)
    cur = qp1 // SEL_BLOCK
    forced = jnp.logical_or(jj == 0, jnp.logical_or(jj == cur, jj == cur - 1))
    score = jnp.where(forced, BIG, jnp.where(jj * SEL_BLOCK <= qp1, imp, NEG))
    ahead = jnp.zeros((nblk, tq), F32)
    for jp in range(nblk):
        sj = score[jp:jp + 1, :]
        first = jnp.logical_or(sj > score, jnp.logical_and(sj == score, jp < jj))
        ahead = ahead + first.astype(F32)
    selb = jnp.where(ahead < float(min(N_SEL, nblk)), 0.0, NEG)
    selb = jnp.concatenate([selb] * HPG, axis=1)
    for j in range(nblk):
        selb_ref[j] = selb[j:j + 1, :]

    kc = SLC_KC
    bpc = kc // SEL_BLOCK

    n = t0 // kc + 1
    ndiag = kc // tq
    sbuf_ref[1] = jnp.full((kc, nq), 2.0 * NEG, F32)
    ebuf_ref[...] = jnp.zeros_like(ebuf_ref)

    def pv(c, slot):
        k0 = pl.multiple_of(jnp.maximum(c, 0) * kc, kc)
        return _dot(vst_ref[0, 0, :, pl.ds(k0, kc)], ebuf_ref[slot])

    def step(i, carry):
        m, l, acc, alpha_p = carry
        acc = alpha_p * acc + pv(i - 2, i % 2)
        sp = sbuf_ref[(i + 1) % 2]
        m_new = jnp.maximum(m, jnp.max(sp, axis=0, keepdims=True))
        alpha = jnp.exp2(m - m_new)
        e = jnp.exp2(sp - m_new)
        l = alpha * l + jnp.sum(e, axis=0, keepdims=True)
        ebuf_ref[(i + 1) % 2] = e.astype(BF16)
        ca = jnp.minimum(i, n - 1)
        k0 = pl.multiple_of(ca * kc, kc)
        s = _dot(ks_ref[0, 0, pl.ds(k0, kc), :], qrt)
        extra = diagb_ref[jnp.where(i == n - 1, (t0 - k0) // tq, ndiag)]
        j0 = ca * bpc
        sbuf_ref[i % 2] = jnp.concatenate(
            [s[bb * SEL_BLOCK:(bb + 1) * SEL_BLOCK, :] + selb_ref[j0 + bb] for bb in range(bpc)],
            axis=0) + extra
        return m_new, l, acc, alpha

    init = (jnp.full((1, nq), NEG, F32), jnp.zeros((1, nq), F32), jnp.zeros((HEAD_DIM, nq), F32),
            jnp.ones((1, nq), F32))
    _, l, acc, alpha_p = lax.fori_loop(0, n + 1, step, init)
    acc = alpha_p * acc + pv(n - 1, (n + 1) % 2)
    o_slc = acc * (1.0 / l)

    wk = WINDOW + tq
    nwin = WINDOW // tq
    k0 = pl.multiple_of(jnp.maximum(t0 - WINDOW, 0), tq)
    s = _dot(kw_ref[0, 0, pl.ds(k0, wk), :], qrt) + winb_ref[jnp.minimum(it, nwin)]
    pw = _softmax2_cols(s)
    o_win = _dot(vwt_ref[0, 0, :, pl.ds(k0, wk)], pw.astype(BF16))

    gtt = gt_ref[0].T

    def gate(k):
        return jnp.concatenate([gtt[h * 3 + k:h * 3 + k + 1, :] for h in range(HPG)], axis=1)

    o = gate(0) * o_cmp + gate(1) * o_slc + gate(2) * o_win
    o = jnp.concatenate([o[:, h * tq:(h + 1) * tq] for h in range(HPG)], axis=0)
    o_ref[0] = o.T.astype(BF16)


def _attn_masks(tq):
    nq = HPG * tq
    lq = (jnp.arange(nq) % tq)[None, :]
    kr = jnp.arange(SLC_KC)[:, None]
    diag = [jnp.where(kr <= d * tq + lq, 0.0, NEG) for d in range(SLC_KC // tq)]
    diag = jnp.stack(diag + [jnp.zeros((SLC_KC, nq))])
    kr = jnp.arange(WINDOW + tq)[:, None]
    win = []
    for w in range(WINDOW // tq + 1):
        delta = w * tq + lq - kr
        win.append(jnp.where((delta >= 0) & (delta < WINDOW), 0.0, NEG))
    return diag.astype(F32), jnp.stack(win).astype(F32)


def _nsa_attention(q, qr, gates, cmp, cmp_t, ks, vst, kw, vwt, ovt, masks):
    bsz, seq, _ = q.shape
    ng, dh = ks.shape[1], ks.shape[3]
    tq = min(ATT_TQ, seq)
    nq = HPG * tq
    nblk = seq // SEL_BLOCK
    ncp = cmp.shape[3]
    diagb, winb = masks
    qspec = pl.BlockSpec((1, tq, HPG * dh), lambda b, g, i: (b, i, g))
    kern = functools.partial(_attn_kernel, tq=tq, nblk=nblk)
    return pl.pallas_call(
        kern,
        grid=(bsz, ng, seq // tq),
        in_specs=[
            qspec, qspec,
            pl.BlockSpec((1, tq, LANES), lambda b, g, i: (b, i, g)),
            pl.BlockSpec((1, 1, 1, ncp, dh), lambda b, g, i: (0, b, g, 0, 0)),
            pl.BlockSpec((1, 1, 1, dh, ncp), lambda b, g, i: (1, b, g, 0, 0)),
            pl.BlockSpec((1, 1, seq, dh), lambda b, g, i: (b, g, 0, 0)),
            pl.BlockSpec((1, 1, dh, seq), lambda b, g, i: (b, g, 0, 0)),
            pl.BlockSpec((1, 1, seq, dh), lambda b, g, i: (b, g, 0, 0)),
            pl.BlockSpec((1, 1, dh, seq), lambda b, g, i: (b, g, 0, 0)),
            pl.BlockSpec((nblk, ncp), lambda b, g, i: (0, 0)),
            pl.BlockSpec(diagb.shape, lambda b, g, i: (0, 0, 0)),
            pl.BlockSpec(winb.shape, lambda b, g, i: (0, 0, 0)),
        ],
        out_specs=qspec,
        out_shape=jax.ShapeDtypeStruct((bsz, seq, ng * HPG * dh), BF16),
        scratch_shapes=[pltpu.VMEM((nblk, 1, nq), F32), pltpu.VMEM((2, SLC_KC, nq), F32),
                        pltpu.VMEM((2, SLC_KC, nq), BF16)],
        compiler_params=_cparams(("parallel", "parallel", "arbitrary")),
        name="nsa_attention",
    )(q, qr, gates, cmp, cmp_t, ks, vst, kw, vwt, ovt, diagb, winb)


def _overlap_t(seq):
    ncp = seq // CMP_STRIDE
    nblk = seq // SEL_BLOCK
    c_start = jnp.arange(ncp) * CMP_STRIDE
    j_start = jnp.arange(nblk) * SEL_BLOCK
    ov = (c_start[None, :] < j_start[:, None] + SEL_BLOCK) & (c_start[None, :] + CMP_BLOCK > j_start[:, None])
    return ov.astype(F32)


def _shared_kv(h3, w_kv, cmp_pos, cmp_w1, cmp_w2, tabs):
    raw, ks, vst, kw, vwt = _kv_proj(h3, w_kv, tabs)
    cmp, cmp_t = _compress(raw, cmp_w1, cmp_pos, cmp_w2)
    return cmp, cmp_t, ks, vst, kw, vwt


def _nsa_layer(h2, w_qg, w_o, shared, tabs, ovt, masks, g, b, bsz, seq):
    q, qr, gates = _qg_proj(h2.reshape(bsz, seq, -1), w_qg, tabs)
    o = _nsa_attention(q, qr, gates, *shared, ovt, masks)
    return _mm_postnorm(o.reshape(bsz * seq, -1), w_o, h2, g, b)


def kernel(x, ln_g, ln_b, lru_w_in, lru_conv_w, lru_conv_b, lru_w_a, lru_b_a, lru_w_i, lru_b_i, lru_lambda,
           lru_w_out, nsa_w_kv, nsa_cmp_pos, nsa_cmp_w1, nsa_cmp_w2, nsa_w_qg, nsa_w_o, ffn_w_gu, ffn_w_down,
           moe_w_router, moe_w_gu, moe_w_down):
    bsz, seq, d = x.shape
    tabs = _rope_tables(seq)
    ovt = _overlap_t(seq)
    masks = _attn_masks(min(ATT_TQ, seq))
    h = x
    shared = None
    for l in range(DEPTH):
        if l < N_A_LAYERS:
            h = _lru_layer(h, lru_w_in[l], lru_conv_w[l], lru_conv_b[l], lru_w_a[l], lru_b_a[l].reshape(-1),
                           lru_w_i[l], lru_b_i[l].reshape(-1), lru_lambda[l], lru_w_out[l],
                           ln_g[l, 0], ln_b[l, 0])
            h2 = h.reshape(bsz * seq, d)
        else:
            lb = l - N_A_LAYERS
            h2 = _nsa_layer(h2, nsa_w_qg[lb], nsa_w_o[lb], shared, tabs, ovt, masks, ln_g[l, 0], ln_b[l, 0],
                            bsz, seq)
        if l % 2 == 0:
            h2 = _dense_ffn(h2, ffn_w_gu[l // 2], ffn_w_down[l // 2], ln_g[l, 1], ln_b[l, 1])
        else:
            h2 = _moe(h2, moe_w_router[l // 2], moe_w_gu[l // 2], moe_w_down[l // 2], ln_g[l, 1], ln_b[l, 1])
        h = h2.reshape(bsz, seq, d)
        if l == N_A_LAYERS - 1:
            shared = _shared_kv(h, nsa_w_kv, nsa_cmp_pos, nsa_cmp_w1, nsa_cmp_w2, tabs)
    return h
```

```python
import functools

import jax
import jax.numpy as jnp
from jax import lax
from jax.experimental import pallas as pl
from jax.experimental.pallas import tpu as pltpu

F32 = jnp.float32
BF16 = jnp.bfloat16
I32 = jnp.int32

DEPTH = 4
N_A_LAYERS = DEPTH // 2
LRU_BLOCKS = 8
CONV_WIDTH = 4
LRU_C = 8.0
N_HEADS = 16
N_KV = 4
HPG = N_HEADS // N_KV
HEAD_DIM = 64
ROT_DIM = HEAD_DIM // 4
ROPE_THETA = 500000.0
CMP_BLOCK = 32
CMP_STRIDE = 16
SEL_BLOCK = 64
N_SEL = 8
WINDOW = 256
N_EXPERTS = 8
DN_ALPHA = (2.0 * DEPTH) ** 0.25
LN_EPS = 1e-5
NEG = -1e30
BIG = 1e30
LOG2E = 1.4426950408889634

LANES = 128
SUBLANES = 8
VMEM_LIMIT = 52 * 1024 * 1024

LRU_TS = 256
FFN_TM = 1024
ROUTE_TM = 512
ROW_TM = 256
PROJ_TM = 512
ATT_TQ = 256
SLC_KC = 256
DMA_UNROLL = 8
VT_ROWS = 80


def _cparams(sem):
    return pltpu.CompilerParams(dimension_semantics=sem, vmem_limit_bytes=VMEM_LIMIT)


def _dot(a, b):
    return jnp.dot(a, b, preferred_element_type=F32)


def _split(a):
    hi = a.astype(BF16)
    lo = (a - hi.astype(F32)).astype(BF16)
    return hi, lo


def _dot_f32(a, b):
    a_hi, a_lo = _split(a)
    b_hi, b_lo = _split(b)
    return _dot(a_hi, b_hi) + (_dot(a_hi, b_lo) + _dot(a_lo, b_hi))


def _postnorm(res, y, g, b):
    z = DN_ALPHA * res + y
    mu = jnp.mean(z, axis=-1, keepdims=True)
    zc = z - mu
    var = jnp.mean(zc * zc, axis=-1, keepdims=True)
    return zc * lax.rsqrt(var + LN_EPS) * g + b


def _lru_kernel(x_ref, win_ref, cw_ref, cb_ref, wai_ref, ba_ref, bi_ref, lam_ref, wout_ref,
                g_ref, b_ref, o_ref, hcar_ref, xprev_ref, hg_ref, *, ts, width, nblk):
    bw = width // nblk

    @pl.when(pl.program_id(1) == 0)
    def _():
        hcar_ref[...] = jnp.zeros_like(hcar_ref)
        xprev_ref[...] = jnp.zeros_like(xprev_ref)

    x = x_ref[0]
    u = _dot(x.astype(BF16), win_ref[...])
    xc = u[:, width:]
    xcat = jnp.concatenate([xprev_ref[...], xc], axis=0)
    xr = cb_ref[...] + cw_ref[CONV_WIDTH - 1:CONV_WIDTH, :] * xc
    for s in range(1, CONV_WIDTH):
        xr = xr + cw_ref[CONV_WIDTH - 1 - s:CONV_WIDTH - s, :] * xcat[SUBLANES - s:SUBLANES - s + ts]
    xprev_ref[...] = xc[ts - SUBLANES:ts]

    row = lax.broadcasted_iota(I32, (ts // SUBLANES, SUBLANES, bw), 1)
    for n in range(nblk):
        sl = slice(n * bw, (n + 1) * bw)
        xb = xr[:, sl]
        ri = _dot(xb.astype(BF16), wai_ref[n])
        r = jax.nn.sigmoid(ri[:, :bw] + ba_ref[:, sl])
        i = jax.nn.sigmoid(ri[:, bw:] + bi_ref[:, sl])
        lam = lam_ref[:, sl]
        sp = jnp.maximum(-lam, 0.0) + jnp.log(1.0 + jnp.exp(-jnp.abs(lam)))
        log_a = (-LRU_C * r) * sp
        a = jnp.exp(log_a)
        th = jnp.tanh(log_a)
        h = jnp.sqrt(-2.0 * th / (1.0 - th)) * (i * xb)
        a = a.reshape(ts // SUBLANES, SUBLANES, bw)
        h = h.reshape(ts // SUBLANES, SUBLANES, bw)
        d = 1
        while d < SUBLANES:
            keep = row >= d
            a_sh = jnp.where(keep, pltpu.roll(a, d, 1), 1.0)
            h_sh = jnp.where(keep, pltpu.roll(h, d, 1), 0.0)
            h = a * h_sh + h
            a = a * a_sh
            d *= 2
        carry = hcar_ref[:, sl]
        groups = []
        for v in range(ts // SUBLANES):
            hv = h[v] + a[v] * carry
            carry = hv[SUBLANES - 1:SUBLANES, :]
            groups.append(hv)
        h = jnp.concatenate(groups, axis=0)
        hcar_ref[:, sl] = carry
        gate = jax.nn.gelu(u[:, sl])
        hg_ref[:, sl] = (h * gate).astype(BF16)

    y = _dot(hg_ref[...], wout_ref[...])
    o_ref[0] = _postnorm(x, y, g_ref[...], b_ref[...])


def _lru_layer(x, w_in, conv_w, conv_b, w_a, b_a, w_i, b_i, lam, w_out, g, b):
    bsz, seq, d = x.shape
    width = w_out.shape[0]
    nblk = w_a.shape[0]
    ts = min(LRU_TS, seq)
    wai = jnp.concatenate([w_a, w_i], axis=-1).astype(BF16)
    row2 = lambda v: v.reshape(1, -1)
    const2 = lambda bi, ti: (0, 0)
    kern = functools.partial(_lru_kernel, ts=ts, width=width, nblk=nblk)
    return pl.pallas_call(
        kern,
        grid=(bsz, seq // ts),
        in_specs=[
            pl.BlockSpec((1, ts, d), lambda bi, ti: (bi, ti, 0)),
            pl.BlockSpec((d, 2 * width), const2),
            pl.BlockSpec((CONV_WIDTH, width), const2),
            pl.BlockSpec((1, width), const2),
            pl.BlockSpec(wai.shape, lambda bi, ti: (0, 0, 0)),
            pl.BlockSpec((1, width), const2),
            pl.BlockSpec((1, width), const2),
            pl.BlockSpec((1, width), const2),
            pl.BlockSpec((width, d), const2),
            pl.BlockSpec((1, d), const2),
            pl.BlockSpec((1, d), const2),
        ],
        out_specs=pl.BlockSpec((1, ts, d), lambda bi, ti: (bi, ti, 0)),
        out_shape=jax.ShapeDtypeStruct((bsz, seq, d), F32),
        scratch_shapes=[
            pltpu.VMEM((1, width), F32),
            pltpu.VMEM((SUBLANES, width), F32),
            pltpu.VMEM((ts, width), BF16),
        ],
        compiler_params=_cparams(("parallel", "arbitrary")),
        name="lru_layer",
    )(x, w_in.astype(BF16), conv_w, row2(conv_b), wai, row2(b_a), row2(b_i), row2(lam),
      w_out.astype(BF16), row2(g), row2(b))


def _ffn_kernel(te_ref, nu_ref, x_ref, wg_ref, wu_ref, wd_ref, g_ref, b_ref, o_ref, xb_ref, acc_ref,
                *, nf, postnorm):
    i = pl.program_id(0)
    f = pl.program_id(1)
    used = i < nu_ref[0]

    @pl.when(used)
    def _():
        @pl.when(f == 0)
        def _():
            xb_ref[...] = x_ref[...].astype(BF16)
            acc_ref[...] = jnp.zeros_like(acc_ref)

        xb = xb_ref[...]
        gp = _dot(xb, wg_ref[0, 0].astype(BF16))
        up = _dot(xb, wu_ref[0, 0].astype(BF16))
        act = (gp * jax.nn.sigmoid(gp) * up).astype(BF16)
        acc_ref[...] += _dot(act, wd_ref[0, 0].astype(BF16))

        @pl.when(f == nf - 1)
        def _():
            if postnorm:
                o_ref[...] = _postnorm(x_ref[...], acc_ref[...], g_ref[...], b_ref[...])
            else:
                o_ref[...] = acc_ref[...]

    @pl.when(jnp.logical_and(jnp.logical_not(used), f == nf - 1))
    def _():
        o_ref[...] = jnp.zeros_like(o_ref)


def _ffn(x, w_gu, w_down, layer, tile_expert, n_used, g, b, *, postnorm, tf):
    rows, d = x.shape
    ff = w_down.shape[2]
    tm = min(FFN_TM, rows)
    nf = ff // tf
    nt = rows // tm

    def fidx(i, f, nu):
        return jnp.where(i < nu[0], f, nf - 1)

    kern = functools.partial(_ffn_kernel, nf=nf, postnorm=postnorm)
    grid_spec = pltpu.PrefetchScalarGridSpec(
        num_scalar_prefetch=2,
        grid=(nt, nf),
        in_specs=[
            pl.BlockSpec((tm, d), lambda i, f, te, nu: (i, 0)),
            pl.BlockSpec((1, 1, d, tf), lambda i, f, te, nu: (layer, te[i], 0, fidx(i, f, nu))),
            pl.BlockSpec((1, 1, d, tf), lambda i, f, te, nu: (layer, te[i], 0, fidx(i, f, nu) + nf)),
            pl.BlockSpec((1, 1, tf, d), lambda i, f, te, nu: (layer, te[i], fidx(i, f, nu), 0)),
            pl.BlockSpec((1, d), lambda i, f, te, nu: (0, 0)),
            pl.BlockSpec((1, d), lambda i, f, te, nu: (0, 0)),
        ],
        out_specs=pl.BlockSpec((tm, d), lambda i, f, te, nu: (i, 0)),
        scratch_shapes=[pltpu.VMEM((tm, d), BF16), pltpu.VMEM((tm, d), F32)],
    )
    return pl.pallas_call(
        kern,
        grid_spec=grid_spec,
        out_shape=jax.ShapeDtypeStruct((rows, d), F32),
        compiler_params=_cparams(("parallel", "arbitrary")),
        name="ffn_postnorm" if postnorm else "ffn_experts",
    )(tile_expert, n_used, x, w_gu, w_gu, w_down, g.reshape(1, -1), b.reshape(1, -1))


def _dense_ffn(h2, w_gu, w_down, layer, g, b):
    rows = h2.shape[0]
    nt = rows // min(FFN_TM, rows)
    return _ffn(h2, w_gu[:, None], w_down[:, None], layer,
                jnp.zeros((nt,), I32), jnp.full((1,), nt, I32), g, b, postnorm=True, tf=256)


def _router_kernel(h_ref, wr_ref, mi_ref, mw_ref, cnt_ref, run_ref, *, tm, n_exp):
    @pl.when(pl.program_id(0) == 0)
    def _():
        run_ref[...] = jnp.zeros_like(run_ref)

    logits = _dot_f32(h_ref[...], wr_ref[...])
    lane = lax.broadcasted_iota(I32, (tm, LANES), 1)
    lg = jnp.where(lane < n_exp, logits, -jnp.inf)
    m0 = jnp.max(lg, axis=1, keepdims=True)
    i0 = jnp.min(jnp.where(lg == m0, lane, LANES), axis=1, keepdims=True)
    lg1 = jnp.where(lane == i0, -jnp.inf, lg)
    m1 = jnp.max(lg1, axis=1, keepdims=True)
    i1 = jnp.min(jnp.where(lg1 == m1, lane, LANES), axis=1, keepdims=True)
    e1 = jnp.exp(m1 - m0)
    w0 = 1.0 / (1.0 + e1)
    w1 = e1 / (1.0 + e1)

    oh0 = lane == i0
    oh1 = lane == i1
    oh = jnp.logical_or(oh0, oh1).astype(F32)
    tri = (lax.broadcasted_iota(I32, (tm, tm), 0) > lax.broadcasted_iota(I32, (tm, tm), 1)).astype(BF16)
    before = _dot(tri, oh.astype(BF16)) + run_ref[0:1, :]
    r0 = jnp.sum(jnp.where(oh0, before, 0.0), axis=1, keepdims=True).astype(I32)
    r1 = jnp.sum(jnp.where(oh1, before, 0.0), axis=1, keepdims=True).astype(I32)
    run_ref[0:1, :] = run_ref[0:1, :] + jnp.sum(oh, axis=0, keepdims=True)

    mi_ref[...] = jnp.where(lane == 0, i0, jnp.where(lane == 1, i1, jnp.where(lane == 2, r0, r1)))
    mw_ref[...] = jnp.where(lane == 0, w0, w1)
    cnt_ref[...] = run_ref[...]


def _router(h2, w_router):
    rows, d = h2.shape
    n_exp = w_router.shape[1]
    tm = min(ROUTE_TM, rows)
    wr = jnp.pad(w_router, ((0, 0), (0, LANES - n_exp)))
    kern = functools.partial(_router_kernel, tm=tm, n_exp=n_exp)
    return pl.pallas_call(
        kern,
        grid=(rows // tm,),
        in_specs=[pl.BlockSpec((tm, d), lambda i: (i, 0)), pl.BlockSpec((d, LANES), lambda i: (0, 0))],
        out_specs=[
            pl.BlockSpec((tm, LANES), lambda i: (i, 0)),
            pl.BlockSpec((tm, LANES), lambda i: (i, 0)),
            pl.BlockSpec((SUBLANES, LANES), lambda i: (0, 0)),
        ],
        out_shape=[
            jax.ShapeDtypeStruct((rows, LANES), I32),
            jax.ShapeDtypeStruct((rows, LANES), F32),
            jax.ShapeDtypeStruct((SUBLANES, LANES), F32),
        ],
        scratch_shapes=[pltpu.VMEM((SUBLANES, LANES), F32)],
        compiler_params=_cparams(("arbitrary",)),
        name="moe_router",
    )(h2, wr)


def _row_copy(src, dst, sem):
    return pltpu.make_async_copy(src, dst, sem)


def _scatter_kernel(pos_ref, h_ref, zin_ref, xs_ref, sem, *, tm):
    del zin_ref

    def start(r, c):
        row = h_ref.at[pl.ds(r, 1)]
        _row_copy(row, xs_ref.at[pl.ds(pos_ref[0, 0, 2 * r], 1)], sem).start()
        _row_copy(row, xs_ref.at[pl.ds(pos_ref[0, 0, 2 * r + 1], 1)], sem).start()
        return c

    lax.fori_loop(0, tm, start, 0, unroll=DMA_UNROLL)

    def wait(r, c):
        row = h_ref.at[pl.ds(r, 1)]
        _row_copy(row, xs_ref.at[pl.ds(pos_ref[0, 0, 2 * r], 1)], sem).wait()
        _row_copy(row, xs_ref.at[pl.ds(pos_ref[0, 0, 2 * r + 1], 1)], sem).wait()
        return c

    lax.fori_loop(0, tm, wait, 0, unroll=DMA_UNROLL)


def _scatter_rows(h2, pos, n_sorted):
    rows, d = h2.shape
    tm = min(ROW_TM, rows)
    nt = rows // tm
    kern = functools.partial(_scatter_kernel, tm=tm)
    return pl.pallas_call(
        kern,
        grid=(nt,),
        in_specs=[
            pl.BlockSpec((1, 1, 2 * tm), lambda i: (i, 0, 0), memory_space=pltpu.SMEM),
            pl.BlockSpec((tm, d), lambda i: (i, 0)),
            pl.BlockSpec(memory_space=pl.ANY),
        ],
        out_specs=pl.BlockSpec(memory_space=pl.ANY),
        out_shape=jax.ShapeDtypeStruct((n_sorted, d), F32),
        scratch_shapes=[pltpu.SemaphoreType.DMA(())],
        input_output_aliases={2: 0},
        compiler_params=_cparams(("arbitrary",)),
        name="moe_scatter",
    )(pos.reshape(nt, 1, 2 * tm), h2, jnp.zeros((n_sorted, d), F32))


def _combine_kernel(pos_ref, mw_ref, h_ref, ys_ref, g_ref, b_ref, o_ref, y0_ref, y1_ref, sem, *, tm):
    def start(r, c):
        _row_copy(ys_ref.at[pl.ds(pos_ref[0, 0, 2 * r], 1)], y0_ref.at[pl.ds(r, 1)], sem).start()
        _row_copy(ys_ref.at[pl.ds(pos_ref[0, 0, 2 * r + 1], 1)], y1_ref.at[pl.ds(r, 1)], sem).start()
        return c

    lax.fori_loop(0, tm, start, 0, unroll=DMA_UNROLL)

    def wait(r, c):
        _row_copy(ys_ref.at[pl.ds(pos_ref[0, 0, 2 * r], 1)], y0_ref.at[pl.ds(r, 1)], sem).wait()
        _row_copy(ys_ref.at[pl.ds(pos_ref[0, 0, 2 * r + 1], 1)], y1_ref.at[pl.ds(r, 1)], sem).wait()
        return c

    lax.fori_loop(0, tm, wait, 0, unroll=DMA_UNROLL)
    w = mw_ref[...]
    y = w[:, 0:1] * y0_ref[...] + w[:, 1:2] * y1_ref[...]
    o_ref[...] = _postnorm(h_ref[...], y, g_ref[...], b_ref[...])


def _combine_rows(h2, ys, pos, mw, g, b):
    rows, d = h2.shape
    tm = min(ROW_TM, rows)
    nt = rows // tm
    kern = functools.partial(_combine_kernel, tm=tm)
    return pl.pallas_call(
        kern,
        grid=(nt,),
        in_specs=[
            pl.BlockSpec((1, 1, 2 * tm), lambda i: (i, 0, 0), memory_space=pltpu.SMEM),
            pl.BlockSpec((tm, LANES), lambda i: (i, 0)),
            pl.BlockSpec((tm, d), lambda i: (i, 0)),
            pl.BlockSpec(memory_space=pl.ANY),
            pl.BlockSpec((1, d), lambda i: (0, 0)),
            pl.BlockSpec((1, d), lambda i: (0, 0)),
        ],
        out_specs=pl.BlockSpec((tm, d), lambda i: (i, 0)),
        out_shape=jax.ShapeDtypeStruct((rows, d), F32),
        scratch_shapes=[pltpu.VMEM((tm, d), F32), pltpu.VMEM((tm, d), F32), pltpu.SemaphoreType.DMA(())],
        compiler_params=_cparams(("arbitrary",)),
        name="moe_combine",
    )(pos.reshape(nt, 1, 2 * tm), mw, h2, ys, g.reshape(1, -1), b.reshape(1, -1))


def _moe(h2, w_router, w_gu, w_down, layer, g, b):
    rows, d = h2.shape
    n_exp = w_router.shape[1]
    tm = min(FFN_TM, rows)
    mi, mw, cnt = _router(h2, w_router)
    counts = cnt[0, :n_exp].astype(I32)
    tiles_per = (counts + tm - 1) // tm
    tile_end = jnp.cumsum(tiles_per)
    row_start = (tile_end - tiles_per) * tm
    e01 = mi[:, 0:2]
    onehot = e01[:, :, None] == jnp.arange(n_exp, dtype=I32)[None, None, :]
    pos = jnp.sum(jnp.where(onehot, row_start[None, None, :], 0), axis=-1) + mi[:, 2:4]
    n_tiles = (2 * rows) // tm + n_exp
    n_used = tile_end[-1]
    tid = jnp.minimum(jnp.arange(n_tiles, dtype=I32), n_used - 1)
    tile_expert = jnp.sum((tid[:, None] >= tile_end[None, :]).astype(I32), axis=1)
    tile_expert = jnp.minimum(tile_expert, n_exp - 1)

    xs = _scatter_rows(h2, pos, n_tiles * tm)
    ys = _ffn(xs, w_gu, w_down, layer, tile_expert, n_used.reshape(1), g, b, postnorm=False, tf=512)
    return _combine_rows(h2, ys, pos, mw, g, b)


def _rope_cols(x, c, sa, sb):
    half = ROT_DIM // 2
    return x * c + pltpu.roll(x, LANES - half, 1) * sa + pltpu.roll(x, half, 1) * sb


def _rope_tables(seq):
    half = ROT_DIM // 2
    inv = ROPE_THETA ** (-jnp.arange(half, dtype=F32) / half)
    ang = jnp.arange(seq, dtype=F32)[:, None] * inv[None]
    cos, sin = jnp.cos(ang), jnp.sin(ang)
    ones = jnp.ones((seq, HEAD_DIM - ROT_DIM), F32)
    zeros = jnp.zeros((seq, HEAD_DIM - ROT_DIM), F32)
    zh = jnp.zeros((seq, half), F32)
    c = jnp.concatenate([cos, cos, ones], axis=1)
    sa = jnp.concatenate([-sin, zh, zeros], axis=1)
    sb = jnp.concatenate([zh, sin, zeros], axis=1)
    rep = LANES // HEAD_DIM
    return tuple(jnp.tile(t, (1, rep)) for t in (c, sa, sb))


def _kv_kernel(h_ref, w_ref, c_ref, sa_ref, sb_ref, raw_ref, ks_ref, vst_ref, kw_ref, vwt_ref, *, gw):
    res = _dot(h_ref[0].astype(BF16), w_ref[...])
    c, sa, sb = c_ref[...], sa_ref[...], sb_ref[...]
    ng = gw // HEAD_DIM

    def part(p):
        return res[:, p * gw:(p + 1) * gw]

    def roped(p):
        x = part(p)
        return jnp.concatenate(
            [_rope_cols(x[:, j * LANES:(j + 1) * LANES], c, sa, sb) for j in range(gw // LANES)], axis=1)

    for t in range(2):
        x = part(t)
        for g in range(ng):
            raw_ref[t, 0, g] = x[:, g * HEAD_DIM:(g + 1) * HEAD_DIM]
    tm = res.shape[0]
    pos = pl.program_id(1) * tm + lax.broadcasted_iota(I32, (tm, LANES - HEAD_DIM), 0)
    blk_onehot = (pos // SEL_BLOCK == lax.broadcasted_iota(I32, (tm, LANES - HEAD_DIM), 1)).astype(F32)
    ones_rows = (lax.broadcasted_iota(I32, (VT_ROWS - HEAD_DIM, tm), 0) == 0).astype(F32)
    for p, k_ref, vt_ref in ((2, ks_ref, vst_ref), (4, kw_ref, vwt_ref)):
        k = roped(p)
        vt = part(p + 1).T
        for g in range(ng):
            kg = k[:, g * HEAD_DIM:(g + 1) * HEAD_DIM]
            if p == 2:
                kg = jnp.concatenate([kg, blk_onehot], axis=1)
            k_ref[0, g] = kg.astype(BF16)
            vt_ref[0, g] = jnp.concatenate([vt[g * HEAD_DIM:(g + 1) * HEAD_DIM, :], ones_rows], axis=0).astype(BF16)


def _kv_proj(h3, w_kv, tabs):
    bsz, seq, d = h3.shape
    n = w_kv.shape[1]
    gw = n // 6
    ng = gw // HEAD_DIM
    tm = min(PROJ_TM, seq)
    assert seq // SEL_BLOCK <= LANES - HEAD_DIM
    tab_spec = pl.BlockSpec((tm, LANES), lambda b, i: (i, 0))

    def k_spec(w):
        return pl.BlockSpec((1, ng, tm, w), lambda b, i: (b, 0, i, 0))

    def k_shape(w):
        return jax.ShapeDtypeStruct((bsz, ng, seq, w), BF16)

    vt_spec = pl.BlockSpec((1, ng, VT_ROWS, tm), lambda b, i: (b, 0, 0, i))
    vt_shape = jax.ShapeDtypeStruct((bsz, ng, VT_ROWS, seq), BF16)
    kern = functools.partial(_kv_kernel, gw=gw)
    return pl.pallas_call(
        kern,
        grid=(bsz, seq // tm),
        in_specs=[pl.BlockSpec((1, tm, d), lambda b, i: (b, i, 0)), pl.BlockSpec((d, n), lambda b, i: (0, 0)),
                  tab_spec, tab_spec, tab_spec],
        out_specs=[pl.BlockSpec((2, 1, ng, tm, HEAD_DIM), lambda b, i: (0, b, 0, i, 0)),
                   k_spec(LANES), vt_spec, k_spec(HEAD_DIM), vt_spec],
        out_shape=[jax.ShapeDtypeStruct((2, bsz, ng, seq, HEAD_DIM), F32),
                   k_shape(LANES), vt_shape, k_shape(HEAD_DIM), vt_shape],
        compiler_params=_cparams(("parallel", "parallel")),
        name="kv_proj",
    )(h3, w_kv.astype(BF16), *tabs)


def _qg_kernel(h_ref, w_ref, c_ref, sa_ref, sb_ref, q_ref, qr_ref, gt_ref, *, nq):
    res = _dot(h_ref[0].astype(BF16), w_ref[...])
    c, sa, sb = c_ref[...], sa_ref[...], sb_ref[...]
    scale = HEAD_DIM ** -0.5 * LOG2E
    for j in range(nq // LANES):
        x = res[:, j * LANES:(j + 1) * LANES] * scale
        q_ref[0, :, j * LANES:(j + 1) * LANES] = x.astype(BF16)
        qr_ref[0, :, j * LANES:(j + 1) * LANES] = _rope_cols(x, c, sa, sb).astype(BF16)
    gt_ref[0] = jax.nn.sigmoid(res[:, nq:])


def _qg_proj(h3, w_qg, tabs):
    bsz, seq, d = h3.shape
    nq = N_HEADS * HEAD_DIM
    ngate = HPG * 3
    tm = min(PROJ_TM, seq)
    wg = w_qg[:, nq:].reshape(d, N_KV, ngate)
    wg = jnp.pad(wg, ((0, 0), (0, 0), (0, LANES - ngate))).reshape(d, N_KV * LANES)
    w = jnp.concatenate([w_qg[:, :nq], wg], axis=1).astype(BF16)
    n = w.shape[1]
    tab_spec = pl.BlockSpec((tm, LANES), lambda b, i: (i, 0))
    kern = functools.partial(_qg_kernel, nq=nq)
    return pl.pallas_call(
        kern,
        grid=(bsz, seq // tm),
        in_specs=[pl.BlockSpec((1, tm, d), lambda b, i: (b, i, 0)), pl.BlockSpec((d, n), lambda b, i: (0, 0)),
                  tab_spec, tab_spec, tab_spec],
        out_specs=[pl.BlockSpec((1, tm, nq), lambda b, i: (b, i, 0)), pl.BlockSpec((1, tm, nq), lambda b, i: (b, i, 0)),
                   pl.BlockSpec((1, tm, N_KV * LANES), lambda b, i: (b, i, 0))],
        out_shape=[jax.ShapeDtypeStruct((bsz, seq, nq), BF16), jax.ShapeDtypeStruct((bsz, seq, nq), BF16),
                   jax.ShapeDtypeStruct((bsz, seq, N_KV * LANES), F32)],
        compiler_params=_cparams(("parallel", "parallel")),
        name="qg_proj",
    )(h3, w, *tabs)


def _mm_ln_kernel(a_ref, w_ref, res_ref, g_ref, b_ref, o_ref):
    y = _dot(a_ref[...], w_ref[...])
    o_ref[...] = _postnorm(res_ref[...], y, g_ref[...], b_ref[...])


def _mm_postnorm(a, w, res, g, b):
    rows, k = a.shape
    d = w.shape[1]
    tm = min(PROJ_TM, rows)
    return pl.pallas_call(
        _mm_ln_kernel,
        grid=(rows // tm,),
        in_specs=[pl.BlockSpec((tm, k), lambda i: (i, 0)), pl.BlockSpec((k, d), lambda i: (0, 0)),
                  pl.BlockSpec((tm, d), lambda i: (i, 0)),
                  pl.BlockSpec((1, d), lambda i: (0, 0)), pl.BlockSpec((1, d), lambda i: (0, 0))],
        out_specs=pl.BlockSpec((tm, d), lambda i: (i, 0)),
        out_shape=jax.ShapeDtypeStruct((rows, d), F32),
        compiler_params=_cparams(("parallel",)),
        name="out_proj_postnorm",
    )(a, w.astype(BF16), res, g.reshape(1, -1), b.reshape(1, -1))


def _compress_kernel(raw_ref, w1_ref, pe_ref, w2_ref, o_ref, ot_ref, *, rows_per):
    half = w1_ref.shape[1] // 2
    pieces = [raw_ref[0, 0, 0, pl.ds(l, rows_per, stride=CMP_STRIDE), :] for l in range(CMP_STRIDE)]
    x = jnp.concatenate(pieces, axis=1).astype(BF16)
    first = _dot(x, w1_ref[0, :half, :])
    second = _dot(x, w1_ref[0, half:, :])
    bias = _dot(pe_ref[0].astype(BF16), w1_ref[0])[0:1, :]
    hid = first + pltpu.roll(second, rows_per - 1, 0) + bias
    out = _dot(jax.nn.gelu(hid).astype(BF16), w2_ref[0])
    o_ref[0, 0, 0] = out.astype(BF16)
    wide = jnp.concatenate([out, jnp.zeros_like(out)], axis=1)
    ot_ref[0, 0, 0] = wide.T[:out.shape[1], :].astype(BF16)


def _compress(raw, w1, pe, w2):
    _, bsz, ng, seq, dh = raw.shape
    rows_per = seq // CMP_STRIDE
    k2 = w1.shape[1]
    hid = w1.shape[2]
    pe8 = jnp.broadcast_to(pe.reshape(2, 1, k2), (2, SUBLANES, k2))
    kern = functools.partial(_compress_kernel, rows_per=rows_per)
    return pl.pallas_call(
        kern,
        grid=(2, bsz, ng),
        in_specs=[
            pl.BlockSpec((1, 1, 1, seq, dh), lambda t, b, g: (t, b, g, 0, 0)),
            pl.BlockSpec((1, k2, hid), lambda t, b, g: (t, 0, 0)),
            pl.BlockSpec((1, SUBLANES, k2), lambda t, b, g: (t, 0, 0)),
            pl.BlockSpec((1, hid, dh), lambda t, b, g: (t, 0, 0)),
        ],
        out_specs=[pl.BlockSpec((1, 1, 1, rows_per, dh), lambda t, b, g: (t, b, g, 0, 0)),
                   pl.BlockSpec((1, 1, 1, dh, rows_per), lambda t, b, g: (t, b, g, 0, 0))],
        out_shape=[jax.ShapeDtypeStruct((2, bsz, ng, rows_per, dh), BF16),
                   jax.ShapeDtypeStruct((2, bsz, ng, dh, rows_per), BF16)],
        compiler_params=_cparams(("parallel", "parallel", "parallel")),
        name="kv_compress",
    )(raw, w1.astype(BF16), pe8, w2.astype(BF16))


def _softmax2_cols(s):
    m = jnp.max(s, axis=0, keepdims=True)
    e = jnp.exp2(s - m)
    return e * (1.0 / jnp.sum(e, axis=0, keepdims=True))


def _heads_t(x):
    xt = x.astype(F32).T
    return jnp.concatenate([xt[h * HEAD_DIM:(h + 1) * HEAD_DIM, :] for h in range(HPG)], axis=1)


def _attn_kernel(q_ref, qr_ref, gt_ref, kc_ref, vct_ref, ks_ref, vst_ref, kw_ref, vwt_ref, ovt_ref,
                 diagb_ref, winb_ref, o_ref, s0_ref, s1_ref, e0_ref, e1_ref, *, tq, nblk):
    nq = HPG * tq
    it = pl.program_id(2)
    t0 = it * tq
    lane_q = lax.broadcasted_iota(I32, (1, nq), 1)
    qpos = t0 + lane_q % tq
    qt = _heads_t(q_ref[0]).astype(BF16)
    qrt = _heads_t(qr_ref[0]).astype(BF16)

    wk = WINDOW + tq
    nwin = WINDOW // tq
    k0 = pl.multiple_of(jnp.maximum(t0 - WINDOW, 0), tq)
    s = _dot(kw_ref[0, 0, pl.ds(k0, wk), :], qrt) + winb_ref[jnp.minimum(it, nwin)]
    ew = jnp.exp2(s - jnp.max(s, axis=0, keepdims=True))
    ow = _dot(vwt_ref[0, 0, :, pl.ds(k0, wk)], ew.astype(BF16))
    o_win = ow[:HEAD_DIM] * (1.0 / ow[HEAD_DIM:HEAD_DIM + 1])

    ncp = kc_ref.shape[3]
    sc = _dot(kc_ref[0, 0, 0], qt)
    blk_end = lax.broadcasted_iota(I32, (ncp, 1), 0) * CMP_STRIDE + (CMP_BLOCK - 1)
    p = _softmax2_cols(jnp.where(blk_end <= qpos, sc, NEG))
    p = p * (qpos >= CMP_BLOCK - 1).astype(F32)
    o_cmp = _dot(vct_ref[0, 0, 0], p.astype(BF16))

    psum = p[:, 0:tq]
    for hh in range(1, HPG):
        psum = psum + p[:, hh * tq:(hh + 1) * tq]
    p_hi, p_lo = _split(psum)
    ovt = ovt_ref[...].astype(BF16)
    imp = _dot(ovt, p_hi) + _dot(ovt, p_lo)
    jj = lax.broadcasted_iota(I32, (nblk, 1), 0)
    qp1 = t0 + lax.broadcasted_iota(I32, (1, tq), 1)
    cur = qp1 // SEL_BLOCK
    forced = jnp.logical_or(jj == 0, jnp.logical_or(jj == cur, jj == cur - 1))
    score = jnp.where(forced, BIG, jnp.where(jj * SEL_BLOCK <= qp1, imp, NEG))
    ahead = jnp.zeros((nblk, tq), F32)
    for jp in range(nblk):
        sj = score[jp:jp + 1, :]
        first = jnp.logical_or(sj > score, jnp.logical_and(sj == score, jp < jj))
        ahead = ahead + first.astype(F32)
    selb = jnp.where(ahead < float(min(N_SEL, nblk)), 0.0, NEG)
    selb = jnp.concatenate([selb] * HPG, axis=1)
    pad = jnp.zeros((LANES - HEAD_DIM - nblk, nq), F32)
    qx = jnp.concatenate([qrt.astype(F32), selb, pad], axis=0).astype(BF16)

    kc = SLC_KC
    n = t0 // kc + 1
    kd = pl.multiple_of((n - 1) * kc, kc)
    sbufs = (s0_ref, s1_ref)
    ebufs = (e0_ref, e1_ref)
    s0_ref[...] = _dot(ks_ref[0, 0, pl.ds(kd, kc), :], qx) + diagb_ref[(t0 - kd) // tq]
    e1_ref[...] = jnp.zeros((kc, nq), BF16)

    def chunk_start(c):
        return pl.multiple_of(jnp.where(c <= 0, kd, (c - 1) * kc), kc)

    def pv(c, e_ref):
        return _dot(vst_ref[0, 0, :, pl.ds(chunk_start(c), kc)], e_ref[...])

    def step(i, par, carry):
        m, acc, alpha_p = carry
        acc = alpha_p * acc + pv(i - 2, ebufs[par])
        sp = sbufs[1 - par][...]
        m_new = jnp.maximum(m, jnp.max(sp, axis=0, keepdims=True))
        alpha = jnp.exp2(m - m_new)
        ebufs[1 - par][...] = jnp.exp2(sp - m_new).astype(BF16)
        k0 = chunk_start(jnp.minimum(i, n - 1))
        sbufs[par][...] = _dot(ks_ref[0, 0, pl.ds(k0, kc), :], qx)
        return m_new, acc, alpha

    def pair(p, carry):
        return step(2 * p + 2, 0, step(2 * p + 1, 1, carry))

    init = (jnp.full((1, nq), NEG, F32), jnp.zeros((VT_ROWS, nq), F32), jnp.ones((1, nq), F32))
    carry = lax.fori_loop(0, n // 2, pair, init)
    odd = n % 2 == 1
    _, acc, alpha_p = lax.cond(odd, lambda c: step(n, 1, c), lambda c: c, carry)
    acc = alpha_p * acc + jnp.where(odd, pv(n - 1, e0_ref), pv(n - 1, e1_ref))
    o_slc = acc[:HEAD_DIM] * (1.0 / acc[HEAD_DIM:HEAD_DIM + 1])

    gtt = gt_ref[0].T

    def gate(k):
        return jnp.concatenate([gtt[h * 3 + k:h * 3 + k + 1, :] for h in range(HPG)], axis=1)

    o = gate(0) * o_cmp + gate(1) * o_slc + gate(2) * o_win
    o = jnp.concatenate([o[:, h * tq:(h + 1) * tq] for h in range(HPG)], axis=0)
    o_ref[0] = o.T.astype(BF16)


def _attn_masks(tq):
    nq = HPG * tq
    lq = (jnp.arange(nq) % tq)[None, :]
    kr = jnp.arange(SLC_KC)[:, None]
    diag = jnp.stack([jnp.where(kr <= d * tq + lq, 0.0, NEG) for d in range(max(SLC_KC // tq, 1))])
    kr = jnp.arange(WINDOW + tq)[:, None]
    win = []
    for w in range(WINDOW // tq + 1):
        delta = w * tq + lq - kr
        win.append(jnp.where((delta >= 0) & (delta < WINDOW), 0.0, NEG))
    return diag.astype(F32), jnp.stack(win).astype(F32)


def _nsa_attention(q, qr, gates, cmp, cmp_t, ks, vst, kw, vwt, ovt, masks):
    bsz, seq, _ = q.shape
    ng, dh = kw.shape[1], kw.shape[3]
    tq = min(ATT_TQ, seq)
    nq = HPG * tq
    nblk = seq // SEL_BLOCK
    ncp = cmp.shape[3]
    diagb, winb = masks
    qspec = pl.BlockSpec((1, tq, HPG * dh), lambda b, g, i: (b, i, g))
    kern = functools.partial(_attn_kernel, tq=tq, nblk=nblk)
    return pl.pallas_call(
        kern,
        grid=(bsz, ng, seq // tq),
        in_specs=[
            qspec, qspec,
            pl.BlockSpec((1, tq, LANES), lambda b, g, i: (b, i, g)),
            pl.BlockSpec((1, 1, 1, ncp, dh), lambda b, g, i: (0, b, g, 0, 0)),
            pl.BlockSpec((1, 1, 1, dh, ncp), lambda b, g, i: (1, b, g, 0, 0)),
            pl.BlockSpec((1, 1, seq, LANES), lambda b, g, i: (b, g, 0, 0)),
            pl.BlockSpec((1, 1, VT_ROWS, seq), lambda b, g, i: (b, g, 0, 0)),
            pl.BlockSpec((1, 1, seq, dh), lambda b, g, i: (b, g, 0, 0)),
            pl.BlockSpec((1, 1, VT_ROWS, seq), lambda b, g, i: (b, g, 0, 0)),
            pl.BlockSpec((nblk, ncp), lambda b, g, i: (0, 0)),
            pl.BlockSpec(diagb.shape, lambda b, g, i: (0, 0, 0)),
            pl.BlockSpec(winb.shape, lambda b, g, i: (0, 0, 0)),
        ],
        out_specs=qspec,
        out_shape=jax.ShapeDtypeStruct((bsz, seq, ng * HPG * dh), BF16),
        scratch_shapes=[pltpu.VMEM((SLC_KC, nq), F32), pltpu.VMEM((SLC_KC, nq), F32),
                        pltpu.VMEM((SLC_KC, nq), BF16), pltpu.VMEM((SLC_KC, nq), BF16)],
        compiler_params=_cparams(("parallel", "parallel", "arbitrary")),
        name="nsa_attention",
    )(q, qr, gates, cmp, cmp_t, ks, vst, kw, vwt, ovt, diagb, winb)


def _overlap_t(seq):
    ncp = seq // CMP_STRIDE
    nblk = seq // SEL_BLOCK
    c_start = jnp.arange(ncp) * CMP_STRIDE
    j_start = jnp.arange(nblk) * SEL_BLOCK
    ov = (c_start[None, :] < j_start[:, None] + SEL_BLOCK) & (c_start[None, :] + CMP_BLOCK > j_start[:, None])
    return ov.astype(F32)


def _shared_kv(h3, w_kv, cmp_pos, cmp_w1, cmp_w2, tabs):
    raw, ks, vst, kw, vwt = _kv_proj(h3, w_kv, tabs)
    cmp, cmp_t = _compress(raw, cmp_w1, cmp_pos, cmp_w2)
    return cmp, cmp_t, ks, vst, kw, vwt


def _nsa_layer(h2, w_qg, w_o, shared, tabs, ovt, masks, g, b, bsz, seq):
    q, qr, gates = _qg_proj(h2.reshape(bsz, seq, -1), w_qg, tabs)
    o = _nsa_attention(q, qr, gates, *shared, ovt, masks)
    return _mm_postnorm(o.reshape(bsz * seq, -1), w_o, h2, g, b)


def kernel(x, ln_g, ln_b, lru_w_in, lru_conv_w, lru_conv_b, lru_w_a, lru_b_a, lru_w_i, lru_b_i, lru_lambda,
           lru_w_out, nsa_w_kv, nsa_cmp_pos, nsa_cmp_w1, nsa_cmp_w2, nsa_w_qg, nsa_w_o, ffn_w_gu, ffn_w_down,
           moe_w_router, moe_w_gu, moe_w_down):
    bsz, seq, d = x.shape
    tabs = _rope_tables(seq)
    ovt = _overlap_t(seq)
    masks = _attn_masks(min(ATT_TQ, seq))
    h = x
    shared = None
    for l in range(DEPTH):
        if l < N_A_LAYERS:
            h = _lru_layer(h, lru_w_in[l], lru_conv_w[l], lru_conv_b[l], lru_w_a[l], lru_b_a[l].reshape(-1),
                           lru_w_i[l], lru_b_i[l].reshape(-1), lru_lambda[l], lru_w_out[l],
                           ln_g[l, 0], ln_b[l, 0])
            h2 = h.reshape(bsz * seq, d)
        else:
            lb = l - N_A_LAYERS
            h2 = _nsa_layer(h2, nsa_w_qg[lb], nsa_w_o[lb], shared, tabs, ovt, masks, ln_g[l, 0], ln_b[l, 0],
                            bsz, seq)
        if l % 2 == 0:
            h2 = _dense_ffn(h2, ffn_w_gu, ffn_w_down, l // 2, ln_g[l, 1], ln_b[l, 1])
        else:
            h2 = _moe(h2, moe_w_router[l // 2], moe_w_gu, moe_w_down, l // 2, ln_g[l, 1], ln_b[l, 1])
        h = h2.reshape(bsz, seq, d)
        if l == N_A_LAYERS - 1:
            shared = _shared_kv(h, nsa_w_kv, nsa_cmp_pos, nsa_cmp_w1, nsa_cmp_w2, tabs)
    return h
```

```python
import functools

import jax
import jax.numpy as jnp
from jax import lax
from jax.experimental import pallas as pl
from jax.experimental.pallas import tpu as pltpu

F32 = jnp.float32
BF16 = jnp.bfloat16
I32 = jnp.int32

DEPTH = 4
N_A_LAYERS = DEPTH // 2
LRU_BLOCKS = 8
CONV_WIDTH = 4
LRU_C = 8.0
N_HEADS = 16
N_KV = 4
HPG = N_HEADS // N_KV
HEAD_DIM = 64
ROT_DIM = HEAD_DIM // 4
ROPE_THETA = 500000.0
CMP_BLOCK = 32
CMP_STRIDE = 16
SEL_BLOCK = 64
N_SEL = 8
WINDOW = 256
N_EXPERTS = 8
DN_ALPHA = (2.0 * DEPTH) ** 0.25
LN_EPS = 1e-5
NEG = -1e30
BIG = 1e30
LOG2E = 1.4426950408889634

LANES = 128
SUBLANES = 8
VMEM_LIMIT = 52 * 1024 * 1024

LRU_TS = 256
FFN_TM = 1024
DENSE_TM = 512
DENSE_TF = 1408
ROUTE_TM = 512
PROJ_TM = 512
ATT_TQ = 256
SLC_KC = 256
SEG_ALIGN = 16
VT_ROWS = 80


def _cparams(sem):
    return pltpu.CompilerParams(dimension_semantics=sem, vmem_limit_bytes=VMEM_LIMIT)


def _dot(a, b):
    return jnp.dot(a, b, preferred_element_type=F32)


def _split(a):
    hi = a.astype(BF16)
    lo = (a - hi.astype(F32)).astype(BF16)
    return hi, lo


def _dot_f32(a, b):
    a_hi, a_lo = _split(a)
    b_hi, b_lo = _split(b)
    return _dot(a_hi, b_hi) + (_dot(a_hi, b_lo) + _dot(a_lo, b_hi))


def _postnorm(res, y, g, b):
    z = DN_ALPHA * res + y
    mu = jnp.mean(z, axis=-1, keepdims=True)
    zc = z - mu
    var = jnp.mean(zc * zc, axis=-1, keepdims=True)
    return zc * lax.rsqrt(var + LN_EPS) * g + b


def _lru_kernel(x_ref, win_ref, cw_ref, cb_ref, wai_ref, ba_ref, bi_ref, lam_ref, wout_ref,
                g_ref, b_ref, o_ref, hcar_ref, xprev_ref, hg_ref, *, ts, width, nblk):
    bw = width // nblk

    @pl.when(pl.program_id(1) == 0)
    def _():
        hcar_ref[...] = jnp.zeros_like(hcar_ref)
        xprev_ref[...] = jnp.zeros_like(xprev_ref)

    x = x_ref[0]
    u = _dot(x.astype(BF16), win_ref[...])
    xc = u[:, width:]
    xcat = jnp.concatenate([xprev_ref[...], xc], axis=0)
    xr = cb_ref[...] + cw_ref[CONV_WIDTH - 1:CONV_WIDTH, :] * xc
    for s in range(1, CONV_WIDTH):
        xr = xr + cw_ref[CONV_WIDTH - 1 - s:CONV_WIDTH - s, :] * xcat[SUBLANES - s:SUBLANES - s + ts]
    xprev_ref[...] = xc[ts - SUBLANES:ts]

    row = lax.broadcasted_iota(I32, (ts // SUBLANES, SUBLANES, bw), 1)
    for n in range(nblk):
        sl = slice(n * bw, (n + 1) * bw)
        xb = xr[:, sl]
        ri = _dot(xb.astype(BF16), wai_ref[n])
        r = jax.nn.sigmoid(ri[:, :bw] + ba_ref[:, sl])
        i = jax.nn.sigmoid(ri[:, bw:] + bi_ref[:, sl])
        lam = lam_ref[:, sl]
        sp = jnp.maximum(-lam, 0.0) + jnp.log(1.0 + jnp.exp(-jnp.abs(lam)))
        log_a = (-LRU_C * r) * sp
        a = jnp.exp(log_a)
        th = jnp.tanh(log_a)
        h = jnp.sqrt(-2.0 * th / (1.0 - th)) * (i * xb)
        a = a.reshape(ts // SUBLANES, SUBLANES, bw)
        h = h.reshape(ts // SUBLANES, SUBLANES, bw)
        d = 1
        while d < SUBLANES:
            keep = row >= d
            a_sh = jnp.where(keep, pltpu.roll(a, d, 1), 1.0)
            h_sh = jnp.where(keep, pltpu.roll(h, d, 1), 0.0)
            h = a * h_sh + h
            a = a * a_sh
            d *= 2
        carry = hcar_ref[:, sl]
        groups = []
        for v in range(ts // SUBLANES):
            hv = h[v] + a[v] * carry
            carry = hv[SUBLANES - 1:SUBLANES, :]
            groups.append(hv)
        h = jnp.concatenate(groups, axis=0)
        hcar_ref[:, sl] = carry
        gate = jax.nn.gelu(u[:, sl])
        hg_ref[:, sl] = (h * gate).astype(BF16)

    y = _dot(hg_ref[...], wout_ref[...])
    o_ref[0] = _postnorm(x, y, g_ref[...], b_ref[...])


def _lru_layer(x, w_in, conv_w, conv_b, w_a, b_a, w_i, b_i, lam, w_out, g, b):
    bsz, seq, d = x.shape
    width = w_out.shape[0]
    nblk = w_a.shape[0]
    ts = min(LRU_TS, seq)
    wai = jnp.concatenate([w_a, w_i], axis=-1).astype(BF16)
    row2 = lambda v: v.reshape(1, -1)
    const2 = lambda bi, ti: (0, 0)
    kern = functools.partial(_lru_kernel, ts=ts, width=width, nblk=nblk)
    return pl.pallas_call(
        kern,
        grid=(bsz, seq // ts),
        in_specs=[
            pl.BlockSpec((1, ts, d), lambda bi, ti: (bi, ti, 0)),
            pl.BlockSpec((d, 2 * width), const2),
            pl.BlockSpec((CONV_WIDTH, width), const2),
            pl.BlockSpec((1, width), const2),
            pl.BlockSpec(wai.shape, lambda bi, ti: (0, 0, 0)),
            pl.BlockSpec((1, width), const2),
            pl.BlockSpec((1, width), const2),
            pl.BlockSpec((1, width), const2),
            pl.BlockSpec((width, d), const2),
            pl.BlockSpec((1, d), const2),
            pl.BlockSpec((1, d), const2),
        ],
        out_specs=pl.BlockSpec((1, ts, d), lambda bi, ti: (bi, ti, 0)),
        out_shape=jax.ShapeDtypeStruct((bsz, seq, d), F32),
        scratch_shapes=[
            pltpu.VMEM((1, width), F32),
            pltpu.VMEM((SUBLANES, width), F32),
            pltpu.VMEM((ts, width), BF16),
        ],
        compiler_params=_cparams(("parallel", "arbitrary")),
        name="lru_layer",
    )(x, w_in.astype(BF16), conv_w, row2(conv_b), wai, row2(b_a), row2(b_i), row2(lam),
      w_out.astype(BF16), row2(g), row2(b))


def _ffn_kernel(te_ref, nu_ref, x_ref, wg_ref, wu_ref, wd_ref, g_ref, b_ref, o_ref, xb_ref, acc_ref,
                *, nf, postnorm):
    i = pl.program_id(0)
    f = pl.program_id(1)
    used = i < nu_ref[0]

    @pl.when(used)
    def _():
        @pl.when(f == 0)
        def _():
            xb_ref[...] = x_ref[...].astype(BF16)
            acc_ref[...] = jnp.zeros_like(acc_ref)

        xb = xb_ref[...]
        gp = _dot(xb, wg_ref[0, 0].astype(BF16))
        up = _dot(xb, wu_ref[0, 0].astype(BF16))
        act = (gp * jax.nn.sigmoid(gp) * up).astype(BF16)
        acc_ref[...] += _dot(act, wd_ref[0, 0].astype(BF16))

        @pl.when(f == nf - 1)
        def _():
            if postnorm:
                o_ref[...] = _postnorm(x_ref[...], acc_ref[...], g_ref[...], b_ref[...])
            else:
                o_ref[...] = acc_ref[...].astype(o_ref.dtype)

    @pl.when(jnp.logical_and(jnp.logical_not(used), f == nf - 1))
    def _():
        o_ref[...] = jnp.zeros_like(o_ref)


def _ffn(x, w_gu, w_down, layer, tile_expert, n_used, g, b, *, postnorm, tf, tm):
    rows, d = x.shape
    ff = w_down.shape[2]
    nf = ff // tf
    nt = rows // tm

    def fidx(i, f, nu):
        return jnp.where(i < nu[0], f, nf - 1)

    kern = functools.partial(_ffn_kernel, nf=nf, postnorm=postnorm)
    grid_spec = pltpu.PrefetchScalarGridSpec(
        num_scalar_prefetch=2,
        grid=(nt, nf),
        in_specs=[
            pl.BlockSpec((tm, d), lambda i, f, te, nu: (i, 0)),
            pl.BlockSpec((1, 1, d, tf), lambda i, f, te, nu: (layer, te[i], 0, fidx(i, f, nu))),
            pl.BlockSpec((1, 1, d, tf), lambda i, f, te, nu: (layer, te[i], 0, fidx(i, f, nu) + nf)),
            pl.BlockSpec((1, 1, tf, d), lambda i, f, te, nu: (layer, te[i], fidx(i, f, nu), 0)),
            pl.BlockSpec((1, d), lambda i, f, te, nu: (0, 0)),
            pl.BlockSpec((1, d), lambda i, f, te, nu: (0, 0)),
        ],
        out_specs=pl.BlockSpec((tm, d), lambda i, f, te, nu: (i, 0)),
        scratch_shapes=[pltpu.VMEM((tm, d), BF16), pltpu.VMEM((tm, d), F32)],
    )
    return pl.pallas_call(
        kern,
        grid_spec=grid_spec,
        out_shape=jax.ShapeDtypeStruct((rows, d), F32 if postnorm else BF16),
        compiler_params=_cparams(("parallel", "arbitrary")),
        name="ffn_postnorm" if postnorm else "ffn_experts",
    )(tile_expert, n_used, x, w_gu, w_gu, w_down, g.reshape(1, -1), b.reshape(1, -1))


def _dense_ffn(h2, w_gu, w_down, layer, g, b):
    rows = h2.shape[0]
    tm = min(DENSE_TM, rows)
    nt = rows // tm
    ff = w_down.shape[1]
    tf = DENSE_TF if ff % DENSE_TF == 0 else LANES
    return _ffn(h2, w_gu[:, None].astype(BF16), w_down[:, None].astype(BF16), layer,
                jnp.zeros((nt,), I32), jnp.full((1,), nt, I32), g, b, postnorm=True, tf=tf, tm=tm)


def _router_kernel(h_ref, wr_ref, ut_ref, xl_ref, mi_ref, mw_ref, seg_ref, *, tm, n_exp, loc):
    x = h_ref[...]
    logits = _dot_f32(x, wr_ref[...])
    lane = lax.broadcasted_iota(I32, (tm, LANES), 1)
    lg = jnp.where(lane < n_exp, logits, -jnp.inf)
    m0 = jnp.max(lg, axis=1, keepdims=True)
    i0 = jnp.min(jnp.where(lg == m0, lane, LANES), axis=1, keepdims=True)
    lg1 = jnp.where(lane == i0, -jnp.inf, lg)
    m1 = jnp.max(lg1, axis=1, keepdims=True)
    i1 = jnp.min(jnp.where(lg1 == m1, lane, LANES), axis=1, keepdims=True)
    e1 = jnp.exp(m1 - m0)
    w0 = 1.0 / (1.0 + e1)
    w1 = e1 / (1.0 + e1)

    oh0 = lane == i0
    oh1 = lane == i1
    oh = jnp.logical_or(oh0, oh1).astype(F32)
    tri = (lax.broadcasted_iota(I32, (tm, tm), 0) > lax.broadcasted_iota(I32, (tm, tm), 1)).astype(BF16)
    before = _dot(tri, oh.astype(BF16))
    cnt = jnp.sum(oh, axis=0, keepdims=True)
    padded = jnp.floor((cnt + (SEG_ALIGN - 1)) * (1.0 / SEG_ALIGN)) * SEG_ALIGN
    seg = _dot(jnp.broadcast_to(padded, (SUBLANES, LANES)).astype(BF16), ut_ref[...].astype(BF16))[0:1, :]
    lp0 = jnp.sum(jnp.where(oh0, seg + before, 0.0), axis=1, keepdims=True)
    lp1 = jnp.sum(jnp.where(oh1, seg + before, 0.0), axis=1, keepdims=True)
    lpt = jnp.where(lane == 0, lp0, jnp.where(lane == 1, lp1, -1.0)).T
    prow = lax.broadcasted_iota(I32, (loc, tm), 0).astype(F32)
    place = jnp.logical_or(prow == lpt[0:1, :], prow == lpt[1:2, :]).astype(BF16)
    xl_ref[0] = _dot(place, x.astype(BF16)).astype(BF16)

    mi_ref[...] = jnp.where(lane == 0, lp0, lp1).astype(I32)
    mw_ref[...] = jnp.where(lane == 0, w0, w1)
    row8 = lax.broadcasted_iota(I32, (SUBLANES, LANES), 0)
    seg_ref[0] = jnp.where(row8 == 0, cnt, seg).astype(I32)


def _router(h2, w_router):
    rows, d = h2.shape
    n_exp = w_router.shape[1]
    tm = min(ROUTE_TM, rows)
    nt = rows // tm
    loc = 2 * tm + LANES
    assert n_exp * (SEG_ALIGN - 1) <= LANES
    wr = jnp.pad(w_router, ((0, 0), (0, LANES - n_exp)))
    upper = (jnp.arange(LANES)[:, None] < jnp.arange(LANES)[None, :]).astype(F32)
    kern = functools.partial(_router_kernel, tm=tm, n_exp=n_exp, loc=loc)
    return pl.pallas_call(
        kern,
        grid=(nt,),
        in_specs=[pl.BlockSpec((tm, d), lambda i: (i, 0)), pl.BlockSpec((d, LANES), lambda i: (0, 0)),
                  pl.BlockSpec((LANES, LANES), lambda i: (0, 0))],
        out_specs=[
            pl.BlockSpec((1, loc, d), lambda i: (i, 0, 0)),
            pl.BlockSpec((tm, LANES), lambda i: (i, 0)),
            pl.BlockSpec((tm, LANES), lambda i: (i, 0)),
            pl.BlockSpec((1, SUBLANES, LANES), lambda i: (i, 0, 0)),
        ],
        out_shape=[
            jax.ShapeDtypeStruct((nt, loc, d), BF16),
            jax.ShapeDtypeStruct((rows, LANES), I32),
            jax.ShapeDtypeStruct((rows, LANES), F32),
            jax.ShapeDtypeStruct((nt, SUBLANES, LANES), I32),
        ],
        compiler_params=_cparams(("parallel",)),
        name="moe_router",
    )(h2, wr, upper)


def _segment_copies(src_ref, dst_ref, units_ref, a_ref, o_ref, sem, seg_index, nbits, do):
    src = src_ref[seg_index]
    dst = dst_ref[seg_index]
    units = units_ref[seg_index]
    off = 0
    for k in reversed(range(nbits)):
        size = SEG_ALIGN << k
        bit = (units >> k) & 1

        @pl.when(bit == 1)
        def _(off=off, size=size):
            do(pltpu.make_async_copy(a_ref.at[pl.ds(pl.multiple_of(src + off, SEG_ALIGN), size)],
                                     o_ref.at[pl.ds(pl.multiple_of(dst + off, SEG_ALIGN), size)], sem))

        off = off + bit * size


def _mover_kernel(src_ref, dst_ref, units_ref, a_ref, zin_ref, o_ref, sem, *, n_exp, nbits):
    del zin_ref
    i = pl.program_id(0)
    for do in (lambda cp: cp.start(), lambda cp: cp.wait()):
        for e in range(n_exp):
            _segment_copies(src_ref, dst_ref, units_ref, a_ref, o_ref, sem, i * n_exp + e, nbits, do)


def _move_segments(a, src, dst, units, n_out, nbits):
    nt, n_exp = src.shape
    d = a.shape[1]
    kern = functools.partial(_mover_kernel, n_exp=n_exp, nbits=nbits)
    grid_spec = pltpu.PrefetchScalarGridSpec(
        num_scalar_prefetch=3,
        grid=(nt,),
        in_specs=[pl.BlockSpec(memory_space=pl.ANY), pl.BlockSpec(memory_space=pl.ANY)],
        out_specs=pl.BlockSpec(memory_space=pl.ANY),
        scratch_shapes=[pltpu.SemaphoreType.DMA(())],
    )
    return pl.pallas_call(
        kern,
        grid_spec=grid_spec,
        out_shape=jax.ShapeDtypeStruct((n_out, d), a.dtype),
        input_output_aliases={4: 0},
        compiler_params=_cparams(("arbitrary",)),
        name="moe_move_segments",
    )(src.reshape(-1), dst.reshape(-1), units.reshape(-1), a, jnp.zeros((n_out, d), a.dtype))


def _combine_kernel(mi_ref, mw_ref, h_ref, yl_ref, g_ref, b_ref, o_ref, *, tm, loc):
    mi = mi_ref[...]
    w = mw_ref[...]
    col = lax.broadcasted_iota(I32, (tm, loc), 1)
    wmat = jnp.where(col == mi[:, 0:1], w[:, 0:1], 0.0) + jnp.where(col == mi[:, 1:2], w[:, 1:2], 0.0)
    w_hi, w_lo = _split(wmat)
    yl = yl_ref[0]
    y = _dot(w_hi, yl) + _dot(w_lo, yl)
    o_ref[...] = _postnorm(h_ref[...], y, g_ref[...], b_ref[...])


def _combine(h2, y_loc, mi, mw, g, b):
    rows, d = h2.shape
    nt, loc, _ = y_loc.shape
    tm = rows // nt
    kern = functools.partial(_combine_kernel, tm=tm, loc=loc)
    return pl.pallas_call(
        kern,
        grid=(nt,),
        in_specs=[
            pl.BlockSpec((tm, LANES), lambda i: (i, 0)),
            pl.BlockSpec((tm, LANES), lambda i: (i, 0)),
            pl.BlockSpec((tm, d), lambda i: (i, 0)),
            pl.BlockSpec((1, loc, d), lambda i: (i, 0, 0)),
            pl.BlockSpec((1, d), lambda i: (0, 0)),
            pl.BlockSpec((1, d), lambda i: (0, 0)),
        ],
        out_specs=pl.BlockSpec((tm, d), lambda i: (i, 0)),
        out_shape=jax.ShapeDtypeStruct((rows, d), F32),
        compiler_params=_cparams(("parallel",)),
        name="moe_combine",
    )(mi, mw, h2, y_loc, g.reshape(1, -1), b.reshape(1, -1))


def _moe(h2, w_router, w_gu, w_down, layer, g, b):
    rows, d = h2.shape
    n_exp = w_router.shape[1]
    tm = min(FFN_TM, rows)
    x_loc, mi, mw, seg = _router(h2, w_router)
    nt_r, loc, _ = x_loc.shape
    counts = seg[:, 0, :n_exp]
    seg_start = seg[:, 1, :n_exp]
    units = (counts + SEG_ALIGN - 1) // SEG_ALIGN
    seg_len = units * SEG_ALIGN
    group_rows = jnp.sum(seg_len, axis=0)
    tiles_per = (group_rows + tm - 1) // tm
    tile_end = jnp.cumsum(tiles_per)
    group_start = (tile_end - tiles_per) * tm
    sorted_start = group_start[None, :] + jnp.cumsum(seg_len, axis=0) - seg_len
    local_start = jnp.arange(nt_r, dtype=I32)[:, None] * loc + seg_start
    n_tiles = -(-(2 * rows + nt_r * n_exp * (SEG_ALIGN - 1)) // tm) + n_exp
    n_used = tile_end[-1]
    tid = jnp.minimum(jnp.arange(n_tiles, dtype=I32), n_used - 1)
    tile_expert = jnp.sum((tid[:, None] >= tile_end[None, :]).astype(I32), axis=1)
    tile_expert = jnp.minimum(tile_expert, n_exp - 1)
    nbits = (min(ROUTE_TM, rows) // SEG_ALIGN).bit_length()

    xs = _move_segments(x_loc.reshape(nt_r * loc, d), local_start, sorted_start, units, n_tiles * tm, nbits)
    ys = _ffn(xs, w_gu, w_down, layer, tile_expert, n_used.reshape(1), g, b, postnorm=False, tf=512, tm=tm)
    y_loc = _move_segments(ys, sorted_start, local_start, units, nt_r * loc, nbits)
    return _combine(h2, y_loc.reshape(nt_r, loc, d), mi, mw, g, b)


def _rope_cols(x, c, sa, sb):
    half = ROT_DIM // 2
    return x * c + pltpu.roll(x, LANES - half, 1) * sa + pltpu.roll(x, half, 1) * sb


def _rope_tables(seq):
    half = ROT_DIM // 2
    inv = ROPE_THETA ** (-jnp.arange(half, dtype=F32) / half)
    ang = jnp.arange(seq, dtype=F32)[:, None] * inv[None]
    cos, sin = jnp.cos(ang), jnp.sin(ang)
    ones = jnp.ones((seq, HEAD_DIM - ROT_DIM), F32)
    zeros = jnp.zeros((seq, HEAD_DIM - ROT_DIM), F32)
    zh = jnp.zeros((seq, half), F32)
    c = jnp.concatenate([cos, cos, ones], axis=1)
    sa = jnp.concatenate([-sin, zh, zeros], axis=1)
    sb = jnp.concatenate([zh, sin, zeros], axis=1)
    rep = LANES // HEAD_DIM
    return tuple(jnp.tile(t, (1, rep)) for t in (c, sa, sb))


def _kv_kernel(h_ref, w_ref, c_ref, sa_ref, sb_ref, raw_ref, ks_ref, vst_ref, kw_ref, vwt_ref, *, gw):
    res = _dot(h_ref[0].astype(BF16), w_ref[...])
    c, sa, sb = c_ref[...], sa_ref[...], sb_ref[...]
    ng = gw // HEAD_DIM

    def part(p):
        return res[:, p * gw:(p + 1) * gw]

    def roped(p):
        x = part(p)
        return jnp.concatenate(
            [_rope_cols(x[:, j * LANES:(j + 1) * LANES], c, sa, sb) for j in range(gw // LANES)], axis=1)

    for t in range(2):
        x = part(t)
        for g in range(ng):
            raw_ref[t, 0, g] = x[:, g * HEAD_DIM:(g + 1) * HEAD_DIM]
    tm = res.shape[0]
    pos = pl.program_id(1) * tm + lax.broadcasted_iota(I32, (tm, LANES - HEAD_DIM), 0)
    blk_onehot = (pos // SEL_BLOCK == lax.broadcasted_iota(I32, (tm, LANES - HEAD_DIM), 1)).astype(F32)
    ones_rows = (lax.broadcasted_iota(I32, (VT_ROWS - HEAD_DIM, tm), 0) == 0).astype(F32)
    for p, k_ref, vt_ref in ((2, ks_ref, vst_ref), (4, kw_ref, vwt_ref)):
        k = roped(p)
        vt = part(p + 1).T
        for g in range(ng):
            kg = k[:, g * HEAD_DIM:(g + 1) * HEAD_DIM]
            if p == 2:
                kg = jnp.concatenate([kg, blk_onehot], axis=1)
            k_ref[0, g] = kg.astype(BF16)
            vt_ref[0, g] = jnp.concatenate([vt[g * HEAD_DIM:(g + 1) * HEAD_DIM, :], ones_rows], axis=0).astype(BF16)


def _kv_proj(h3, w_kv, tabs):
    bsz, seq, d = h3.shape
    n = w_kv.shape[1]
    gw = n // 6
    ng = gw // HEAD_DIM
    tm = min(PROJ_TM, seq)
    assert seq // SEL_BLOCK <= LANES - HEAD_DIM
    tab_spec = pl.BlockSpec((tm, LANES), lambda b, i: (i, 0))

    def k_spec(w):
        return pl.BlockSpec((1, ng, tm, w), lambda b, i: (b, 0, i, 0))

    def k_shape(w):
        return jax.ShapeDtypeStruct((bsz, ng, seq, w), BF16)

    vt_spec = pl.BlockSpec((1, ng, VT_ROWS, tm), lambda b, i: (b, 0, 0, i))
    vt_shape = jax.ShapeDtypeStruct((bsz, ng, VT_ROWS, seq), BF16)
    kern = functools.partial(_kv_kernel, gw=gw)
    return pl.pallas_call(
        kern,
        grid=(bsz, seq // tm),
        in_specs=[pl.BlockSpec((1, tm, d), lambda b, i: (b, i, 0)), pl.BlockSpec((d, n), lambda b, i: (0, 0)),
                  tab_spec, tab_spec, tab_spec],
        out_specs=[pl.BlockSpec((2, 1, ng, tm, HEAD_DIM), lambda b, i: (0, b, 0, i, 0)),
                   k_spec(LANES), vt_spec, k_spec(HEAD_DIM), vt_spec],
        out_shape=[jax.ShapeDtypeStruct((2, bsz, ng, seq, HEAD_DIM), F32),
                   k_shape(LANES), vt_shape, k_shape(HEAD_DIM), vt_shape],
        compiler_params=_cparams(("parallel", "parallel")),
        name="kv_proj",
    )(h3, w_kv.astype(BF16), *tabs)


def _qg_kernel(h_ref, w_ref, c_ref, sa_ref, sb_ref, q_ref, qr_ref, gt_ref, *, nq):
    res = _dot(h_ref[0].astype(BF16), w_ref[...])
    c, sa, sb = c_ref[...], sa_ref[...], sb_ref[...]
    scale = HEAD_DIM ** -0.5 * LOG2E
    for j in range(nq // LANES):
        x = res[:, j * LANES:(j + 1) * LANES] * scale
        q_ref[0, :, j * LANES:(j + 1) * LANES] = x.astype(BF16)
        qr_ref[0, :, j * LANES:(j + 1) * LANES] = _rope_cols(x, c, sa, sb).astype(BF16)
    gt_ref[0] = jax.nn.sigmoid(res[:, nq:])


def _qg_proj(h3, w_qg, tabs):
    bsz, seq, d = h3.shape
    nq = N_HEADS * HEAD_DIM
    ngate = HPG * 3
    tm = min(PROJ_TM, seq)
    wg = w_qg[:, nq:].reshape(d, N_KV, ngate)
    wg = jnp.pad(wg, ((0, 0), (0, 0), (0, LANES - ngate))).reshape(d, N_KV * LANES)
    w = jnp.concatenate([w_qg[:, :nq], wg], axis=1).astype(BF16)
    n = w.shape[1]
    tab_spec = pl.BlockSpec((tm, LANES), lambda b, i: (i, 0))
    kern = functools.partial(_qg_kernel, nq=nq)
    return pl.pallas_call(
        kern,
        grid=(bsz, seq // tm),
        in_specs=[pl.BlockSpec((1, tm, d), lambda b, i: (b, i, 0)), pl.BlockSpec((d, n), lambda b, i: (0, 0)),
                  tab_spec, tab_spec, tab_spec],
        out_specs=[pl.BlockSpec((1, tm, nq), lambda b, i: (b, i, 0)), pl.BlockSpec((1, tm, nq), lambda b, i: (b, i, 0)),
                   pl.BlockSpec((1, tm, N_KV * LANES), lambda b, i: (b, i, 0))],
        out_shape=[jax.ShapeDtypeStruct((bsz, seq, nq), BF16), jax.ShapeDtypeStruct((bsz, seq, nq), BF16),
                   jax.ShapeDtypeStruct((bsz, seq, N_KV * LANES), F32)],
        compiler_params=_cparams(("parallel", "parallel")),
        name="qg_proj",
    )(h3, w, *tabs)


def _mm_ln_kernel(a_ref, w_ref, res_ref, g_ref, b_ref, o_ref):
    y = _dot(a_ref[...], w_ref[...])
    o_ref[...] = _postnorm(res_ref[...], y, g_ref[...], b_ref[...])


def _mm_postnorm(a, w, res, g, b):
    rows, k = a.shape
    d = w.shape[1]
    tm = min(PROJ_TM, rows)
    return pl.pallas_call(
        _mm_ln_kernel,
        grid=(rows // tm,),
        in_specs=[pl.BlockSpec((tm, k), lambda i: (i, 0)), pl.BlockSpec((k, d), lambda i: (0, 0)),
                  pl.BlockSpec((tm, d), lambda i: (i, 0)),
                  pl.BlockSpec((1, d), lambda i: (0, 0)), pl.BlockSpec((1, d), lambda i: (0, 0))],
        out_specs=pl.BlockSpec((tm, d), lambda i: (i, 0)),
        out_shape=jax.ShapeDtypeStruct((rows, d), F32),
        compiler_params=_cparams(("parallel",)),
        name="out_proj_postnorm",
    )(a, w.astype(BF16), res, g.reshape(1, -1), b.reshape(1, -1))


def _compress_kernel(raw_ref, w1_ref, pe_ref, w2_ref, o_ref, ot_ref, *, rows_per):
    half = w1_ref.shape[1] // 2
    pieces = [raw_ref[0, 0, 0, pl.ds(l, rows_per, stride=CMP_STRIDE), :] for l in range(CMP_STRIDE)]
    x = jnp.concatenate(pieces, axis=1).astype(BF16)
    first = _dot(x, w1_ref[0, :half, :])
    second = _dot(x, w1_ref[0, half:, :])
    bias = _dot(pe_ref[0].astype(BF16), w1_ref[0])[0:1, :]
    hid = first + pltpu.roll(second, rows_per - 1, 0) + bias
    out = _dot(jax.nn.gelu(hid).astype(BF16), w2_ref[0])
    o_ref[0, 0, 0] = out.astype(BF16)
    wide = jnp.concatenate([out, jnp.zeros_like(out)], axis=1)
    ot_ref[0, 0, 0] = wide.T[:out.shape[1], :].astype(BF16)


def _compress(raw, w1, pe, w2):
    _, bsz, ng, seq, dh = raw.shape
    rows_per = seq // CMP_STRIDE
    k2 = w1.shape[1]
    hid = w1.shape[2]
    pe8 = jnp.broadcast_to(pe.reshape(2, 1, k2), (2, SUBLANES, k2))
    kern = functools.partial(_compress_kernel, rows_per=rows_per)
    return pl.pallas_call(
        kern,
        grid=(2, bsz, ng),
        in_specs=[
            pl.BlockSpec((1, 1, 1, seq, dh), lambda t, b, g: (t, b, g, 0, 0)),
            pl.BlockSpec((1, k2, hid), lambda t, b, g: (t, 0, 0)),
            pl.BlockSpec((1, SUBLANES, k2), lambda t, b, g: (t, 0, 0)),
            pl.BlockSpec((1, hid, dh), lambda t, b, g: (t, 0, 0)),
        ],
        out_specs=[pl.BlockSpec((1, 1, 1, rows_per, dh), lambda t, b, g: (t, b, g, 0, 0)),
                   pl.BlockSpec((1, 1, 1, dh, rows_per), lambda t, b, g: (t, b, g, 0, 0))],
        out_shape=[jax.ShapeDtypeStruct((2, bsz, ng, rows_per, dh), BF16),
                   jax.ShapeDtypeStruct((2, bsz, ng, dh, rows_per), BF16)],
        compiler_params=_cparams(("parallel", "parallel", "parallel")),
        name="kv_compress",
    )(raw, w1.astype(BF16), pe8, w2.astype(BF16))


def _softmax2_cols(s):
    m = jnp.max(s, axis=0, keepdims=True)
    e = jnp.exp2(s - m)
    return e * (1.0 / jnp.sum(e, axis=0, keepdims=True))


def _heads_t(x):
    xt = x.astype(F32).T
    return jnp.concatenate([xt[h * HEAD_DIM:(h + 1) * HEAD_DIM, :] for h in range(HPG)], axis=1)


def _attn_kernel(q_ref, qr_ref, gt_ref, kc_ref, vct_ref, ks_ref, vst_ref, kw_ref, vwt_ref, ovt_ref,
                 diagb_ref, winb_ref, o_ref, s0_ref, s1_ref, e0_ref, e1_ref, *, tq, nblk):
    nq = HPG * tq
    it = pl.program_id(2)
    t0 = it * tq
    lane_q = lax.broadcasted_iota(I32, (1, nq), 1)
    qpos = t0 + lane_q % tq
    qt = _heads_t(q_ref[0]).astype(BF16)
    qrt = _heads_t(qr_ref[0]).astype(BF16)

    wk = WINDOW + tq
    nwin = WINDOW // tq
    k0 = pl.multiple_of(jnp.maximum(t0 - WINDOW, 0), tq)
    s = _dot(kw_ref[0, 0, pl.ds(k0, wk), :], qrt) + winb_ref[jnp.minimum(it, nwin)]
    ew = jnp.exp2(s - jnp.max(s, axis=0, keepdims=True))
    ow = _dot(vwt_ref[0, 0, :, pl.ds(k0, wk)], ew.astype(BF16))
    o_win = ow[:HEAD_DIM] * (1.0 / ow[HEAD_DIM:HEAD_DIM + 1])

    ncp = kc_ref.shape[3]
    sc = _dot(kc_ref[0, 0, 0], qt)
    blk_end = lax.broadcasted_iota(I32, (ncp, 1), 0) * CMP_STRIDE + (CMP_BLOCK - 1)
    p = _softmax2_cols(jnp.where(blk_end <= qpos, sc, NEG))
    p = p * (qpos >= CMP_BLOCK - 1).astype(F32)
    o_cmp = _dot(vct_ref[0, 0, 0], p.astype(BF16))

    psum = p[:, 0:tq]
    for hh in range(1, HPG):
        psum = psum + p[:, hh * tq:(hh + 1) * tq]
    p_hi, p_lo = _split(psum)
    ovt = ovt_ref[...].astype(BF16)
    imp = _dot(ovt, p_hi) + _dot(ovt, p_lo)
    jj = lax.broadcasted_iota(I32, (nblk, 1), 0)
    qp1 = t0 + lax.broadcasted_iota(I32, (1, tq), 1)
    cur = qp1 // SEL_BLOCK
    forced = jnp.logical_or(jj == 0, jnp.logical_or(jj == cur, jj == cur - 1))
    score = jnp.where(forced, BIG, jnp.where(jj * SEL_BLOCK <= qp1, imp, NEG))
    ahead = jnp.zeros((nblk, tq), F32)
    for jp in range(nblk):
        sj = score[jp:jp + 1, :]
        first = jnp.logical_or(sj > score, jnp.logical_and(sj == score, jp < jj))
        ahead = ahead + first.astype(F32)
    selb = jnp.where(ahead < float(min(N_SEL, nblk)), 0.0, NEG)
    selb = jnp.concatenate([selb] * HPG, axis=1)
    pad = jnp.zeros((LANES - HEAD_DIM - nblk, nq), F32)
    qx = jnp.concatenate([qrt.astype(F32), selb, pad], axis=0).astype(BF16)

    kc = SLC_KC
    n = t0 // kc + 1
    kd = pl.multiple_of((n - 1) * kc, kc)
    sbufs = (s0_ref, s1_ref)
    ebufs = (e0_ref, e1_ref)
    s0_ref[...] = _dot(ks_ref[0, 0, pl.ds(kd, kc), :], qx) + diagb_ref[(t0 - kd) // tq]
    e1_ref[...] = jnp.zeros((kc, nq), BF16)

    def chunk_start(c):
        return pl.multiple_of(jnp.where(c <= 0, kd, (c - 1) * kc), kc)

    def pv(c, e_ref):
        return _dot(vst_ref[0, 0, :, pl.ds(chunk_start(c), kc)], e_ref[...])

    def step(i, par, carry):
        m, acc, alpha_p = carry
        acc = alpha_p * acc + pv(i - 2, ebufs[par])
        sp = sbufs[1 - par][...]
        m_new = jnp.maximum(m, jnp.max(sp, axis=0, keepdims=True))
        alpha = jnp.exp2(m - m_new)
        ebufs[1 - par][...] = jnp.exp2(sp - m_new).astype(BF16)
        k0 = chunk_start(jnp.minimum(i, n - 1))
        sbufs[par][...] = _dot(ks_ref[0, 0, pl.ds(k0, kc), :], qx)
        return m_new, acc, alpha

    def pair(p, carry):
        return step(2 * p + 2, 0, step(2 * p + 1, 1, carry))

    init = (jnp.full((1, nq), NEG, F32), jnp.zeros((VT_ROWS, nq), F32), jnp.ones((1, nq), F32))
    carry = lax.fori_loop(0, n // 2, pair, init)
    odd = n % 2 == 1
    _, acc, alpha_p = lax.cond(odd, lambda c: step(n, 1, c), lambda c: c, carry)
    acc = alpha_p * acc + jnp.where(odd, pv(n - 1, e0_ref), pv(n - 1, e1_ref))
    o_slc = acc[:HEAD_DIM] * (1.0 / acc[HEAD_DIM:HEAD_DIM + 1])

    gtt = gt_ref[0].T

    def gate(k):
        return jnp.concatenate([gtt[h * 3 + k:h * 3 + k + 1, :] for h in range(HPG)], axis=1)

    o = gate(0) * o_cmp + gate(1) * o_slc + gate(2) * o_win
    o = jnp.concatenate([o[:, h * tq:(h + 1) * tq] for h in range(HPG)], axis=0)
    o_ref[0] = o.T.astype(BF16)


def _attn_masks(tq):
    nq = HPG * tq
    lq = (jnp.arange(nq) % tq)[None, :]
    kr = jnp.arange(SLC_KC)[:, None]
    diag = jnp.stack([jnp.where(kr <= d * tq + lq, 0.0, NEG) for d in range(max(SLC_KC // tq, 1))])
    kr = jnp.arange(WINDOW + tq)[:, None]
    win = []
    for w in range(WINDOW // tq + 1):
        delta = w * tq + lq - kr
        win.append(jnp.where((delta >= 0) & (delta < WINDOW), 0.0, NEG))
    return diag.astype(F32), jnp.stack(win).astype(F32)


def _nsa_attention(q, qr, gates, cmp, cmp_t, ks, vst, kw, vwt, ovt, masks):
    bsz, seq, _ = q.shape
    ng, dh = kw.shape[1], kw.shape[3]
    tq = min(ATT_TQ, seq)
    nq = HPG * tq
    nblk = seq // SEL_BLOCK
    ncp = cmp.shape[3]
    diagb, winb = masks
    qspec = pl.BlockSpec((1, tq, HPG * dh), lambda b, g, i: (b, i, g))
    kern = functools.partial(_attn_kernel, tq=tq, nblk=nblk)
    return pl.pallas_call(
        kern,
        grid=(bsz, ng, seq // tq),
        in_specs=[
            qspec, qspec,
            pl.BlockSpec((1, tq, LANES), lambda b, g, i: (b, i, g)),
            pl.BlockSpec((1, 1, 1, ncp, dh), lambda b, g, i: (0, b, g, 0, 0)),
            pl.BlockSpec((1, 1, 1, dh, ncp), lambda b, g, i: (1, b, g, 0, 0)),
            pl.BlockSpec((1, 1, seq, LANES), lambda b, g, i: (b, g, 0, 0)),
            pl.BlockSpec((1, 1, VT_ROWS, seq), lambda b, g, i: (b, g, 0, 0)),
            pl.BlockSpec((1, 1, seq, dh), lambda b, g, i: (b, g, 0, 0)),
            pl.BlockSpec((1, 1, VT_ROWS, seq), lambda b, g, i: (b, g, 0, 0)),
            pl.BlockSpec((nblk, ncp), lambda b, g, i: (0, 0)),
            pl.BlockSpec(diagb.shape, lambda b, g, i: (0, 0, 0)),
            pl.BlockSpec(winb.shape, lambda b, g, i: (0, 0, 0)),
        ],
        out_specs=qspec,
        out_shape=jax.ShapeDtypeStruct((bsz, seq, ng * HPG * dh), BF16),
        scratch_shapes=[pltpu.VMEM((SLC_KC, nq), F32), pltpu.VMEM((SLC_KC, nq), F32),
                        pltpu.VMEM((SLC_KC, nq), BF16), pltpu.VMEM((SLC_KC, nq), BF16)],
        compiler_params=_cparams(("parallel", "parallel", "arbitrary")),
        name="nsa_attention",
    )(q, qr, gates, cmp, cmp_t, ks, vst, kw, vwt, ovt, diagb, winb)


def _overlap_t(seq):
    ncp = seq // CMP_STRIDE
    nblk = seq // SEL_BLOCK
    c_start = jnp.arange(ncp) * CMP_STRIDE
    j_start = jnp.arange(nblk) * SEL_BLOCK
    ov = (c_start[None, :] < j_start[:, None] + SEL_BLOCK) & (c_start[None, :] + CMP_BLOCK > j_start[:, None])
    return ov.astype(F32)


def _shared_kv(h3, w_kv, cmp_pos, cmp_w1, cmp_w2, tabs):
    raw, ks, vst, kw, vwt = _kv_proj(h3, w_kv, tabs)
    cmp, cmp_t = _compress(raw, cmp_w1, cmp_pos, cmp_w2)
    return cmp, cmp_t, ks, vst, kw, vwt


def _nsa_layer(h2, w_qg, w_o, shared, tabs, ovt, masks, g, b, bsz, seq):
    q, qr, gates = _qg_proj(h2.reshape(bsz, seq, -1), w_qg, tabs)
    o = _nsa_attention(q, qr, gates, *shared, ovt, masks)
    return _mm_postnorm(o.reshape(bsz * seq, -1), w_o, h2, g, b)


def kernel(x, ln_g, ln_b, lru_w_in, lru_conv_w, lru_conv_b, lru_w_a, lru_b_a, lru_w_i, lru_b_i, lru_lambda,
           lru_w_out, nsa_w_kv, nsa_cmp_pos, nsa_cmp_w1, nsa_cmp_w2, nsa_w_qg, nsa_w_o, ffn_w_gu, ffn_w_down,
           moe_w_router, moe_w_gu, moe_w_down):
    bsz, seq, d = x.shape
    tabs = _rope_tables(seq)
    ovt = _overlap_t(seq)
    masks = _attn_masks(min(ATT_TQ, seq))
    h = x
    shared = None
    for l in range(DEPTH):
        if l < N_A_LAYERS:
            h = _lru_layer(h, lru_w_in[l], lru_conv_w[l], lru_conv_b[l], lru_w_a[l], lru_b_a[l].reshape(-1),
                           lru_w_i[l], lru_b_i[l].reshape(-1), lru_lambda[l], lru_w_out[l],
                           ln_g[l, 0], ln_b[l, 0])
            h2 = h.reshape(bsz * seq, d)
        else:
            lb = l - N_A_LAYERS
            h2 = _nsa_layer(h2, nsa_w_qg[lb], nsa_w_o[lb], shared, tabs, ovt, masks, ln_g[l, 0], ln_b[l, 0],
                            bsz, seq)
        if l % 2 == 0:
            h2 = _dense_ffn(h2, ffn_w_gu, ffn_w_down, l // 2, ln_g[l, 1], ln_b[l, 1])
        else:
            h2 = _moe(h2, moe_w_router[l // 2], moe_w_gu, moe_w_down, l // 2, ln_g[l, 1], ln_b[l, 1])
        h = h2.reshape(bsz, seq, d)
        if l == N_A_LAYERS - 1:
            shared = _shared_kv(h, nsa_w_kv, nsa_cmp_pos, nsa_cmp_w1, nsa_cmp_w2, tabs)
    return h
```

```python
import functools

import jax
import jax.numpy as jnp
from jax import lax
from jax.experimental import pallas as pl
from jax.experimental.pallas import tpu as pltpu

F32 = jnp.float32
BF16 = jnp.bfloat16
I32 = jnp.int32

DEPTH = 4
N_A_LAYERS = DEPTH // 2
LRU_BLOCKS = 8
CONV_WIDTH = 4
LRU_C = 8.0
N_HEADS = 16
N_KV = 4
HPG = N_HEADS // N_KV
HEAD_DIM = 64
ROT_DIM = HEAD_DIM // 4
ROPE_THETA = 500000.0
CMP_BLOCK = 32
CMP_STRIDE = 16
SEL_BLOCK = 64
N_SEL = 8
WINDOW = 256
N_EXPERTS = 8
DN_ALPHA = (2.0 * DEPTH) ** 0.25
LN_EPS = 1e-5
NEG = -1e30
BIG = 1e30
LOG2E = 1.4426950408889634

LANES = 128
SUBLANES = 8
VMEM_LIMIT = 52 * 1024 * 1024

LRU_TS = 256
FFN_TM = 1024
DENSE_TM = 512
DENSE_TF = 1408
ROUTE_TM = 512
PROJ_TM = 512
ATT_TQ = 256
SLC_KC = 256
SEG_ALIGN = 16
VT_ROWS = 80


def _cparams(sem):
    return pltpu.CompilerParams(dimension_semantics=sem, vmem_limit_bytes=VMEM_LIMIT)


def _dot(a, b):
    return jnp.dot(a, b, preferred_element_type=F32)


def _split(a):
    hi = a.astype(BF16)
    lo = (a - hi.astype(F32)).astype(BF16)
    return hi, lo


def _dot_f32(a, b):
    a_hi, a_lo = _split(a)
    b_hi, b_lo = _split(b)
    return _dot(a_hi, b_hi) + (_dot(a_hi, b_lo) + _dot(a_lo, b_hi))


def _postnorm(res, y, g, b):
    z = DN_ALPHA * res + y
    mu = jnp.mean(z, axis=-1, keepdims=True)
    zc = z - mu
    var = jnp.mean(zc * zc, axis=-1, keepdims=True)
    return zc * lax.rsqrt(var + LN_EPS) * g + b


def _lru_kernel(x_ref, win_ref, cw_ref, cb_ref, wai_ref, ba_ref, bi_ref, lam_ref, wout_ref,
                g_ref, b_ref, o_ref, hcar_ref, xprev_ref, hg_ref, *, ts, width, nblk):
    bw = width // nblk

    @pl.when(pl.program_id(1) == 0)
    def _():
        hcar_ref[...] = jnp.zeros_like(hcar_ref)
        xprev_ref[...] = jnp.zeros_like(xprev_ref)

    x = x_ref[0]
    u = _dot(x.astype(BF16), win_ref[...])
    xc = u[:, width:]
    xcat = jnp.concatenate([xprev_ref[...], xc], axis=0)
    xr = cb_ref[...] + cw_ref[CONV_WIDTH - 1:CONV_WIDTH, :] * xc
    for s in range(1, CONV_WIDTH):
        xr = xr + cw_ref[CONV_WIDTH - 1 - s:CONV_WIDTH - s, :] * xcat[SUBLANES - s:SUBLANES - s + ts]
    xprev_ref[...] = xc[ts - SUBLANES:ts]

    row = lax.broadcasted_iota(I32, (ts // SUBLANES, SUBLANES, bw), 1)
    for n in range(nblk):
        sl = slice(n * bw, (n + 1) * bw)
        xb = xr[:, sl]
        ri = _dot(xb.astype(BF16), wai_ref[n])
        r = jax.nn.sigmoid(ri[:, :bw] + ba_ref[:, sl])
        i = jax.nn.sigmoid(ri[:, bw:] + bi_ref[:, sl])
        lam = lam_ref[:, sl]
        sp = jnp.maximum(-lam, 0.0) + jnp.log(1.0 + jnp.exp(-jnp.abs(lam)))
        log_a = (-LRU_C * r) * sp
        a = jnp.exp(log_a)
        th = jnp.tanh(log_a)
        h = jnp.sqrt(-2.0 * th / (1.0 - th)) * (i * xb)
        a = a.reshape(ts // SUBLANES, SUBLANES, bw)
        h = h.reshape(ts // SUBLANES, SUBLANES, bw)
        d = 1
        while d < SUBLANES:
            keep = row >= d
            a_sh = jnp.where(keep, pltpu.roll(a, d, 1), 1.0)
            h_sh = jnp.where(keep, pltpu.roll(h, d, 1), 0.0)
            h = a * h_sh + h
            a = a * a_sh
            d *= 2
        carry = hcar_ref[:, sl]
        groups = []
        for v in range(ts // SUBLANES):
            hv = h[v] + a[v] * carry
            carry = hv[SUBLANES - 1:SUBLANES, :]
            groups.append(hv)
        h = jnp.concatenate(groups, axis=0)
        hcar_ref[:, sl] = carry
        gate = jax.nn.gelu(u[:, sl])
        hg_ref[:, sl] = (h * gate).astype(BF16)

    y = _dot(hg_ref[...], wout_ref[...])
    o_ref[0] = _postnorm(x, y, g_ref[...], b_ref[...])


def _lru_layer(x, w_in, conv_w, conv_b, w_a, b_a, w_i, b_i, lam, w_out, g, b):
    bsz, seq, d = x.shape
    width = w_out.shape[0]
    nblk = w_a.shape[0]
    ts = min(LRU_TS, seq)
    wai = jnp.concatenate([w_a, w_i], axis=-1).astype(BF16)
    row2 = lambda v: v.reshape(1, -1)
    const2 = lambda bi, ti: (0, 0)
    kern = functools.partial(_lru_kernel, ts=ts, width=width, nblk=nblk)
    return pl.pallas_call(
        kern,
        grid=(bsz, seq // ts),
        in_specs=[
            pl.BlockSpec((1, ts, d), lambda bi, ti: (bi, ti, 0)),
            pl.BlockSpec((d, 2 * width), const2),
            pl.BlockSpec((CONV_WIDTH, width), const2),
            pl.BlockSpec((1, width), const2),
            pl.BlockSpec(wai.shape, lambda bi, ti: (0, 0, 0)),
            pl.BlockSpec((1, width), const2),
            pl.BlockSpec((1, width), const2),
            pl.BlockSpec((1, width), const2),
            pl.BlockSpec((width, d), const2),
            pl.BlockSpec((1, d), const2),
            pl.BlockSpec((1, d), const2),
        ],
        out_specs=pl.BlockSpec((1, ts, d), lambda bi, ti: (bi, ti, 0)),
        out_shape=jax.ShapeDtypeStruct((bsz, seq, d), F32),
        scratch_shapes=[
            pltpu.VMEM((1, width), F32),
            pltpu.VMEM((SUBLANES, width), F32),
            pltpu.VMEM((ts, width), BF16),
        ],
        compiler_params=_cparams(("parallel", "arbitrary")),
        name="lru_layer",
    )(x, w_in.astype(BF16), conv_w, row2(conv_b), wai, row2(b_a), row2(b_i), row2(lam),
      w_out.astype(BF16), row2(g), row2(b))


def _ffn_kernel(te_ref, nu_ref, x_ref, wg_ref, wu_ref, wd_ref, g_ref, b_ref, o_ref, xb_ref, acc_ref,
                *, nf, postnorm):
    i = pl.program_id(0)
    f = pl.program_id(1)
    used = i < nu_ref[0]

    @pl.when(used)
    def _():
        @pl.when(f == 0)
        def _():
            xb_ref[...] = x_ref[...].astype(BF16)
            acc_ref[...] = jnp.zeros_like(acc_ref)

        xb = xb_ref[...]
        gp = _dot(xb, wg_ref[0, 0].astype(BF16))
        up = _dot(xb, wu_ref[0, 0].astype(BF16))
        act = (gp * jax.nn.sigmoid(gp) * up).astype(BF16)
        acc_ref[...] += _dot(act, wd_ref[0, 0].astype(BF16))

        @pl.when(f == nf - 1)
        def _():
            if postnorm:
                o_ref[...] = _postnorm(x_ref[...], acc_ref[...], g_ref[...], b_ref[...])
            else:
                o_ref[...] = acc_ref[...].astype(o_ref.dtype)

    @pl.when(jnp.logical_and(jnp.logical_not(used), f == nf - 1))
    def _():
        o_ref[...] = jnp.zeros_like(o_ref)


def _ffn(x, w_gu, w_down, layer, tile_expert, n_used, g, b, *, postnorm, tf, tm):
    rows, d = x.shape
    ff = w_down.shape[2]
    nf = ff // tf
    nt = rows // tm

    def fidx(i, f, nu):
        return jnp.where(i < nu[0], f, nf - 1)

    kern = functools.partial(_ffn_kernel, nf=nf, postnorm=postnorm)
    grid_spec = pltpu.PrefetchScalarGridSpec(
        num_scalar_prefetch=2,
        grid=(nt, nf),
        in_specs=[
            pl.BlockSpec((tm, d), lambda i, f, te, nu: (i, 0)),
            pl.BlockSpec((1, 1, d, tf), lambda i, f, te, nu: (layer, te[i], 0, fidx(i, f, nu))),
            pl.BlockSpec((1, 1, d, tf), lambda i, f, te, nu: (layer, te[i], 0, fidx(i, f, nu) + nf)),
            pl.BlockSpec((1, 1, tf, d), lambda i, f, te, nu: (layer, te[i], fidx(i, f, nu), 0)),
            pl.BlockSpec((1, d), lambda i, f, te, nu: (0, 0)),
            pl.BlockSpec((1, d), lambda i, f, te, nu: (0, 0)),
        ],
        out_specs=pl.BlockSpec((tm, d), lambda i, f, te, nu: (i, 0)),
        scratch_shapes=[pltpu.VMEM((tm, d), BF16), pltpu.VMEM((tm, d), F32)],
    )
    return pl.pallas_call(
        kern,
        grid_spec=grid_spec,
        out_shape=jax.ShapeDtypeStruct((rows, d), F32 if postnorm else BF16),
        compiler_params=_cparams(("parallel", "arbitrary")),
        name="ffn_postnorm" if postnorm else "ffn_experts",
    )(tile_expert, n_used, x, w_gu, w_gu, w_down, g.reshape(1, -1), b.reshape(1, -1))


def _dense_ffn(h2, w_gu, w_down, layer, g, b):
    rows = h2.shape[0]
    tm = min(DENSE_TM, rows)
    nt = rows // tm
    ff = w_down.shape[1]
    tf = DENSE_TF if ff % DENSE_TF == 0 else LANES
    return _ffn(h2, w_gu[:, None].astype(BF16), w_down[:, None].astype(BF16), layer,
                jnp.zeros((nt,), I32), jnp.full((1,), nt, I32), g, b, postnorm=True, tf=tf, tm=tm)


def _router_kernel(h_ref, wr_ref, ut_ref, xl_ref, mi_ref, mw_ref, seg_ref, *, tm, n_exp, loc):
    x = h_ref[...]
    logits = _dot_f32(x, wr_ref[...])
    lane = lax.broadcasted_iota(I32, (tm, LANES), 1)
    lg = jnp.where(lane < n_exp, logits, -jnp.inf)
    m0 = jnp.max(lg, axis=1, keepdims=True)
    i0 = jnp.min(jnp.where(lg == m0, lane, LANES), axis=1, keepdims=True)
    lg1 = jnp.where(lane == i0, -jnp.inf, lg)
    m1 = jnp.max(lg1, axis=1, keepdims=True)
    i1 = jnp.min(jnp.where(lg1 == m1, lane, LANES), axis=1, keepdims=True)
    e1 = jnp.exp(m1 - m0)
    w0 = 1.0 / (1.0 + e1)
    w1 = e1 / (1.0 + e1)

    oh0 = lane == i0
    oh1 = lane == i1
    oh = jnp.logical_or(oh0, oh1).astype(F32)
    tri = (lax.broadcasted_iota(I32, (tm, tm), 0) > lax.broadcasted_iota(I32, (tm, tm), 1)).astype(BF16)
    before = _dot(tri, oh.astype(BF16))
    cnt = jnp.sum(oh, axis=0, keepdims=True)
    padded = jnp.floor((cnt + (SEG_ALIGN - 1)) * (1.0 / SEG_ALIGN)) * SEG_ALIGN
    seg = _dot(jnp.broadcast_to(padded, (SUBLANES, LANES)).astype(BF16), ut_ref[...].astype(BF16))[0:1, :]
    lp0 = jnp.sum(jnp.where(oh0, seg + before, 0.0), axis=1, keepdims=True)
    lp1 = jnp.sum(jnp.where(oh1, seg + before, 0.0), axis=1, keepdims=True)
    lpt = jnp.where(lane == 0, lp0, jnp.where(lane == 1, lp1, -1.0)).T
    prow = lax.broadcasted_iota(I32, (loc, tm), 0).astype(F32)
    place = jnp.logical_or(prow == lpt[0:1, :], prow == lpt[1:2, :]).astype(BF16)
    xl_ref[0] = _dot(place, x.astype(BF16)).astype(BF16)

    mi_ref[...] = jnp.where(lane == 0, lp0, lp1).astype(I32)
    mw_ref[...] = jnp.where(lane == 0, w0, w1)
    row8 = lax.broadcasted_iota(I32, (SUBLANES, LANES), 0)
    seg_ref[0] = jnp.where(row8 == 0, cnt, seg).astype(I32)


def _router(h2, w_router):
    rows, d = h2.shape
    n_exp = w_router.shape[1]
    tm = min(ROUTE_TM, rows)
    nt = rows // tm
    loc = 2 * tm + LANES
    assert n_exp * (SEG_ALIGN - 1) <= LANES
    wr = jnp.pad(w_router, ((0, 0), (0, LANES - n_exp)))
    upper = (jnp.arange(LANES)[:, None] < jnp.arange(LANES)[None, :]).astype(F32)
    kern = functools.partial(_router_kernel, tm=tm, n_exp=n_exp, loc=loc)
    return pl.pallas_call(
        kern,
        grid=(nt,),
        in_specs=[pl.BlockSpec((tm, d), lambda i: (i, 0)), pl.BlockSpec((d, LANES), lambda i: (0, 0)),
                  pl.BlockSpec((LANES, LANES), lambda i: (0, 0))],
        out_specs=[
            pl.BlockSpec((1, loc, d), lambda i: (i, 0, 0)),
            pl.BlockSpec((tm, LANES), lambda i: (i, 0)),
            pl.BlockSpec((tm, LANES), lambda i: (i, 0)),
            pl.BlockSpec((1, SUBLANES, LANES), lambda i: (i, 0, 0)),
        ],
        out_shape=[
            jax.ShapeDtypeStruct((nt, loc, d), BF16),
            jax.ShapeDtypeStruct((rows, LANES), I32),
            jax.ShapeDtypeStruct((rows, LANES), F32),
            jax.ShapeDtypeStruct((nt, SUBLANES, LANES), I32),
        ],
        compiler_params=_cparams(("parallel",)),
        name="moe_router",
    )(h2, wr, upper)


def _segment_copies(loc_ref, srt_ref, units_ref, local_buf, sorted_hbm, sem, n_exp, nbits, to_sorted, do):
    base = pl.program_id(0) * n_exp
    for e in range(n_exp):
        lo = loc_ref[base + e]
        so = srt_ref[base + e]
        units = units_ref[base + e]
        off = 0
        for k in reversed(range(nbits)):
            size = SEG_ALIGN << k
            bit = (units >> k) & 1

            @pl.when(bit == 1)
            def _(off=off, size=size, lo=lo, so=so):
                local = local_buf.at[pl.ds(pl.multiple_of(lo + off, SEG_ALIGN), size)]
                srt = sorted_hbm.at[pl.ds(pl.multiple_of(so + off, SEG_ALIGN), size)]
                do(pltpu.make_async_copy(local, srt, sem) if to_sorted else pltpu.make_async_copy(srt, local, sem))

            off = off + bit * size


def _spread_kernel(loc_ref, srt_ref, units_ref, xl_ref, zin_ref, o_ref, sem, *, n_exp, nbits):
    del zin_ref
    for do in (lambda cp: cp.start(), lambda cp: cp.wait()):
        _segment_copies(loc_ref, srt_ref, units_ref, xl_ref.at[0], o_ref, sem, n_exp, nbits, True, do)


def _spread_segments(x_loc, seg_start, sorted_start, units, n_out, nbits):
    nt, loc, d = x_loc.shape
    n_exp = units.shape[1]
    kern = functools.partial(_spread_kernel, n_exp=n_exp, nbits=nbits)
    grid_spec = pltpu.PrefetchScalarGridSpec(
        num_scalar_prefetch=3,
        grid=(nt,),
        in_specs=[pl.BlockSpec((1, loc, d), lambda i, a, b, c: (i, 0, 0)), pl.BlockSpec(memory_space=pl.ANY)],
        out_specs=pl.BlockSpec(memory_space=pl.ANY),
        scratch_shapes=[pltpu.SemaphoreType.DMA(())],
    )
    return pl.pallas_call(
        kern,
        grid_spec=grid_spec,
        out_shape=jax.ShapeDtypeStruct((n_out, d), x_loc.dtype),
        input_output_aliases={4: 0},
        compiler_params=_cparams(("arbitrary",)),
        name="moe_spread",
    )(seg_start.reshape(-1), sorted_start.reshape(-1), units.reshape(-1), x_loc, jnp.zeros((n_out, d), x_loc.dtype))


def _combine_kernel(loc_ref, srt_ref, units_ref, mi_ref, mw_ref, h_ref, ys_ref, g_ref, b_ref, o_ref, yl_ref, sem,
                    *, tm, n_exp, nbits):
    loc = yl_ref.shape[0]
    yl_ref[...] = jnp.zeros_like(yl_ref)
    for do in (lambda cp: cp.start(), lambda cp: cp.wait()):
        _segment_copies(loc_ref, srt_ref, units_ref, yl_ref, ys_ref, sem, n_exp, nbits, False, do)
    mi = mi_ref[...]
    w = mw_ref[...]
    col = lax.broadcasted_iota(I32, (tm, loc), 1)
    wmat = jnp.where(col == mi[:, 0:1], w[:, 0:1], 0.0) + jnp.where(col == mi[:, 1:2], w[:, 1:2], 0.0)
    w_hi, w_lo = _split(wmat)
    yl = yl_ref[...]
    y = _dot(w_hi, yl) + _dot(w_lo, yl)
    o_ref[...] = _postnorm(h_ref[...], y, g_ref[...], b_ref[...])


def _combine(h2, ys, seg_start, sorted_start, units, loc, mi, mw, g, b, nbits):
    rows, d = h2.shape
    nt, n_exp = units.shape
    tm = rows // nt
    kern = functools.partial(_combine_kernel, tm=tm, n_exp=n_exp, nbits=nbits)
    grid_spec = pltpu.PrefetchScalarGridSpec(
        num_scalar_prefetch=3,
        grid=(nt,),
        in_specs=[
            pl.BlockSpec((tm, LANES), lambda i, a, b, c: (i, 0)),
            pl.BlockSpec((tm, LANES), lambda i, a, b, c: (i, 0)),
            pl.BlockSpec((tm, d), lambda i, a, b, c: (i, 0)),
            pl.BlockSpec(memory_space=pl.ANY),
            pl.BlockSpec((1, d), lambda i, a, b, c: (0, 0)),
            pl.BlockSpec((1, d), lambda i, a, b, c: (0, 0)),
        ],
        out_specs=pl.BlockSpec((tm, d), lambda i, a, b, c: (i, 0)),
        scratch_shapes=[pltpu.VMEM((loc, d), ys.dtype), pltpu.SemaphoreType.DMA(())],
    )
    return pl.pallas_call(
        kern,
        grid_spec=grid_spec,
        out_shape=jax.ShapeDtypeStruct((rows, d), F32),
        compiler_params=_cparams(("arbitrary",)),
        name="moe_combine",
    )(seg_start.reshape(-1), sorted_start.reshape(-1), units.reshape(-1), mi, mw, h2, ys,
      g.reshape(1, -1), b.reshape(1, -1))


def _moe(h2, w_router, w_gu, w_down, layer, g, b):
    rows, d = h2.shape
    n_exp = w_router.shape[1]
    tm = min(FFN_TM, rows)
    x_loc, mi, mw, seg = _router(h2, w_router)
    nt_r, loc, _ = x_loc.shape
    counts = seg[:, 0, :n_exp]
    seg_start = seg[:, 1, :n_exp]
    units = (counts + SEG_ALIGN - 1) // SEG_ALIGN
    seg_len = units * SEG_ALIGN
    group_rows = jnp.sum(seg_len, axis=0)
    tiles_per = (group_rows + tm - 1) // tm
    tile_end = jnp.cumsum(tiles_per)
    group_start = (tile_end - tiles_per) * tm
    sorted_start = group_start[None, :] + jnp.cumsum(seg_len, axis=0) - seg_len
    n_tiles = -(-(2 * rows + nt_r * n_exp * (SEG_ALIGN - 1)) // tm) + n_exp
    n_used = tile_end[-1]
    tid = jnp.minimum(jnp.arange(n_tiles, dtype=I32), n_used - 1)
    tile_expert = jnp.sum((tid[:, None] >= tile_end[None, :]).astype(I32), axis=1)
    tile_expert = jnp.minimum(tile_expert, n_exp - 1)
    nbits = (min(ROUTE_TM, rows) // SEG_ALIGN).bit_length()

    xs = _spread_segments(x_loc, seg_start, sorted_start, units, n_tiles * tm, nbits)
    ys = _ffn(xs, w_gu, w_down, layer, tile_expert, n_used.reshape(1), g, b, postnorm=False, tf=512, tm=tm)
    return _combine(h2, ys, seg_start, sorted_start, units, loc, mi, mw, g, b, nbits)


def _rope_cols(x, c, sa, sb):
    half = ROT_DIM // 2
    return x * c + pltpu.roll(x, LANES - half, 1) * sa + pltpu.roll(x, half, 1) * sb


def _rope_tables(seq):
    half = ROT_DIM // 2
    inv = ROPE_THETA ** (-jnp.arange(half, dtype=F32) / half)
    ang = jnp.arange(seq, dtype=F32)[:, None] * inv[None]
    cos, sin = jnp.cos(ang), jnp.sin(ang)
    ones = jnp.ones((seq, HEAD_DIM - ROT_DIM), F32)
    zeros = jnp.zeros((seq, HEAD_DIM - ROT_DIM), F32)
    zh = jnp.zeros((seq, half), F32)
    c = jnp.concatenate([cos, cos, ones], axis=1)
    sa = jnp.concatenate([-sin, zh, zeros], axis=1)
    sb = jnp.concatenate([zh, sin, zeros], axis=1)
    rep = LANES // HEAD_DIM
    return tuple(jnp.tile(t, (1, rep)) for t in (c, sa, sb))


def _kv_kernel(h_ref, w_ref, c_ref, sa_ref, sb_ref, raw_ref, ks_ref, vst_ref, kw_ref, vwt_ref, *, gw):
    res = _dot(h_ref[0].astype(BF16), w_ref[...])
    c, sa, sb = c_ref[...], sa_ref[...], sb_ref[...]
    ng = gw // HEAD_DIM

    def part(p):
        return res[:, p * gw:(p + 1) * gw]

    def roped(p):
        x = part(p)
        return jnp.concatenate(
            [_rope_cols(x[:, j * LANES:(j + 1) * LANES], c, sa, sb) for j in range(gw // LANES)], axis=1)

    for t in range(2):
        x = part(t)
        for g in range(ng):
            raw_ref[t, 0, g] = x[:, g * HEAD_DIM:(g + 1) * HEAD_DIM]
    tm = res.shape[0]
    pos = pl.program_id(1) * tm + lax.broadcasted_iota(I32, (tm, LANES - HEAD_DIM), 0)
    blk_onehot = (pos // SEL_BLOCK == lax.broadcasted_iota(I32, (tm, LANES - HEAD_DIM), 1)).astype(F32)
    ones_rows = (lax.broadcasted_iota(I32, (VT_ROWS - HEAD_DIM, tm), 0) == 0).astype(F32)
    for p, k_ref, vt_ref in ((2, ks_ref, vst_ref), (4, kw_ref, vwt_ref)):
        k = roped(p)
        vt = part(p + 1).T
        for g in range(ng):
            kg = k[:, g * HEAD_DIM:(g + 1) * HEAD_DIM]
            if p == 2:
                kg = jnp.concatenate([kg, blk_onehot], axis=1)
            k_ref[0, g] = kg.astype(BF16)
            vt_ref[0, g] = jnp.concatenate([vt[g * HEAD_DIM:(g + 1) * HEAD_DIM, :], ones_rows], axis=0).astype(BF16)


def _kv_proj(h3, w_kv, tabs):
    bsz, seq, d = h3.shape
    n = w_kv.shape[1]
    gw = n // 6
    ng = gw // HEAD_DIM
    tm = min(PROJ_TM, seq)
    assert seq // SEL_BLOCK <= LANES - HEAD_DIM
    tab_spec = pl.BlockSpec((tm, LANES), lambda b, i: (i, 0))

    def k_spec(w):
        return pl.BlockSpec((1, ng, tm, w), lambda b, i: (b, 0, i, 0))

    def k_shape(w):
        return jax.ShapeDtypeStruct((bsz, ng, seq, w), BF16)

    vt_spec = pl.BlockSpec((1, ng, VT_ROWS, tm), lambda b, i: (b, 0, 0, i))
    vt_shape = jax.ShapeDtypeStruct((bsz, ng, VT_ROWS, seq), BF16)
    kern = functools.partial(_kv_kernel, gw=gw)
    return pl.pallas_call(
        kern,
        grid=(bsz, seq // tm),
        in_specs=[pl.BlockSpec((1, tm, d), lambda b, i: (b, i, 0)), pl.BlockSpec((d, n), lambda b, i: (0, 0)),
                  tab_spec, tab_spec, tab_spec],
        out_specs=[pl.BlockSpec((2, 1, ng, tm, HEAD_DIM), lambda b, i: (0, b, 0, i, 0)),
                   k_spec(LANES), vt_spec, k_spec(HEAD_DIM), vt_spec],
        out_shape=[jax.ShapeDtypeStruct((2, bsz, ng, seq, HEAD_DIM), F32),
                   k_shape(LANES), vt_shape, k_shape(HEAD_DIM), vt_shape],
        compiler_params=_cparams(("parallel", "parallel")),
        name="kv_proj",
    )(h3, w_kv.astype(BF16), *tabs)


def _qg_kernel(h_ref, w_ref, c_ref, sa_ref, sb_ref, q_ref, qr_ref, gt_ref, *, nq):
    res = _dot(h_ref[0].astype(BF16), w_ref[...])
    c, sa, sb = c_ref[...], sa_ref[...], sb_ref[...]
    scale = HEAD_DIM ** -0.5 * LOG2E
    for j in range(nq // LANES):
        x = res[:, j * LANES:(j + 1) * LANES] * scale
        q_ref[0, :, j * LANES:(j + 1) * LANES] = x.astype(BF16)
        qr_ref[0, :, j * LANES:(j + 1) * LANES] = _rope_cols(x, c, sa, sb).astype(BF16)
    gt_ref[0] = jax.nn.sigmoid(res[:, nq:])


def _qg_proj(h3, w_qg, tabs):
    bsz, seq, d = h3.shape
    nq = N_HEADS * HEAD_DIM
    ngate = HPG * 3
    tm = min(PROJ_TM, seq)
    wg = w_qg[:, nq:].reshape(d, N_KV, ngate)
    wg = jnp.pad(wg, ((0, 0), (0, 0), (0, LANES - ngate))).reshape(d, N_KV * LANES)
    w = jnp.concatenate([w_qg[:, :nq], wg], axis=1).astype(BF16)
    n = w.shape[1]
    tab_spec = pl.BlockSpec((tm, LANES), lambda b, i: (i, 0))
    kern = functools.partial(_qg_kernel, nq=nq)
    return pl.pallas_call(
        kern,
        grid=(bsz, seq // tm),
        in_specs=[pl.BlockSpec((1, tm, d), lambda b, i: (b, i, 0)), pl.BlockSpec((d, n), lambda b, i: (0, 0)),
                  tab_spec, tab_spec, tab_spec],
        out_specs=[pl.BlockSpec((1, tm, nq), lambda b, i: (b, i, 0)), pl.BlockSpec((1, tm, nq), lambda b, i: (b, i, 0)),
                   pl.BlockSpec((1, tm, N_KV * LANES), lambda b, i: (b, i, 0))],
        out_shape=[jax.ShapeDtypeStruct((bsz, seq, nq), BF16), jax.ShapeDtypeStruct((bsz, seq, nq), BF16),
                   jax.ShapeDtypeStruct((bsz, seq, N_KV * LANES), F32)],
        compiler_params=_cparams(("parallel", "parallel")),
        name="qg_proj",
    )(h3, w, *tabs)


def _mm_ln_kernel(a_ref, w_ref, res_ref, g_ref, b_ref, o_ref):
    y = _dot(a_ref[...], w_ref[...])
    o_ref[...] = _postnorm(res_ref[...], y, g_ref[...], b_ref[...])


def _mm_postnorm(a, w, res, g, b):
    rows, k = a.shape
    d = w.shape[1]
    tm = min(PROJ_TM, rows)
    return pl.pallas_call(
        _mm_ln_kernel,
        grid=(rows // tm,),
        in_specs=[pl.BlockSpec((tm, k), lambda i: (i, 0)), pl.BlockSpec((k, d), lambda i: (0, 0)),
                  pl.BlockSpec((tm, d), lambda i: (i, 0)),
                  pl.BlockSpec((1, d), lambda i: (0, 0)), pl.BlockSpec((1, d), lambda i: (0, 0))],
        out_specs=pl.BlockSpec((tm, d), lambda i: (i, 0)),
        out_shape=jax.ShapeDtypeStruct((rows, d), F32),
        compiler_params=_cparams(("parallel",)),
        name="out_proj_postnorm",
    )(a, w.astype(BF16), res, g.reshape(1, -1), b.reshape(1, -1))


def _compress_kernel(raw_ref, w1_ref, pe_ref, w2_ref, o_ref, ot_ref, *, rows_per):
    half = w1_ref.shape[1] // 2
    pieces = [raw_ref[0, 0, 0, pl.ds(l, rows_per, stride=CMP_STRIDE), :] for l in range(CMP_STRIDE)]
    x = jnp.concatenate(pieces, axis=1).astype(BF16)
    first = _dot(x, w1_ref[0, :half, :])
    second = _dot(x, w1_ref[0, half:, :])
    bias = _dot(pe_ref[0].astype(BF16), w1_ref[0])[0:1, :]
    hid = first + pltpu.roll(second, rows_per - 1, 0) + bias
    out = _dot(jax.nn.gelu(hid).astype(BF16), w2_ref[0])
    o_ref[0, 0, 0] = out.astype(BF16)
    wide = jnp.concatenate([out, jnp.zeros_like(out)], axis=1)
    ot_ref[0, 0, 0] = wide.T[:out.shape[1], :].astype(BF16)


def _compress(raw, w1, pe, w2):
    _, bsz, ng, seq, dh = raw.shape
    rows_per = seq // CMP_STRIDE
    k2 = w1.shape[1]
    hid = w1.shape[2]
    pe8 = jnp.broadcast_to(pe.reshape(2, 1, k2), (2, SUBLANES, k2))
    kern = functools.partial(_compress_kernel, rows_per=rows_per)
    return pl.pallas_call(
        kern,
        grid=(2, bsz, ng),
        in_specs=[
            pl.BlockSpec((1, 1, 1, seq, dh), lambda t, b, g: (t, b, g, 0, 0)),
            pl.BlockSpec((1, k2, hid), lambda t, b, g: (t, 0, 0)),
            pl.BlockSpec((1, SUBLANES, k2), lambda t, b, g: (t, 0, 0)),
            pl.BlockSpec((1, hid, dh), lambda t, b, g: (t, 0, 0)),
        ],
        out_specs=[pl.BlockSpec((1, 1, 1, rows_per, dh), lambda t, b, g: (t, b, g, 0, 0)),
                   pl.BlockSpec((1, 1, 1, dh, rows_per), lambda t, b, g: (t, b, g, 0, 0))],
        out_shape=[jax.ShapeDtypeStruct((2, bsz, ng, rows_per, dh), BF16),
                   jax.ShapeDtypeStruct((2, bsz, ng, dh, rows_per), BF16)],
        compiler_params=_cparams(("parallel", "parallel", "parallel")),
        name="kv_compress",
    )(raw, w1.astype(BF16), pe8, w2.astype(BF16))


def _softmax2_cols(s):
    m = jnp.max(s, axis=0, keepdims=True)
    e = jnp.exp2(s - m)
    return e * (1.0 / jnp.sum(e, axis=0, keepdims=True))


def _heads_t(x):
    xt = x.astype(F32).T
    return jnp.concatenate([xt[h * HEAD_DIM:(h + 1) * HEAD_DIM, :] for h in range(HPG)], axis=1)


def _attn_kernel(q_ref, qr_ref, gt_ref, kc_ref, vct_ref, ks_ref, vst_ref, kw_ref, vwt_ref, ovt_ref,
                 diagb_ref, winb_ref, o_ref, s0_ref, s1_ref, e0_ref, e1_ref, *, tq, nblk):
    nq = HPG * tq
    it = pl.program_id(2)
    t0 = it * tq
    lane_q = lax.broadcasted_iota(I32, (1, nq), 1)
    qpos = t0 + lane_q % tq
    qt = _heads_t(q_ref[0]).astype(BF16)
    qrt = _heads_t(qr_ref[0]).astype(BF16)

    wk = WINDOW + tq
    nwin = WINDOW // tq
    k0 = pl.multiple_of(jnp.maximum(t0 - WINDOW, 0), tq)
    s = _dot(kw_ref[0, 0, pl.ds(k0, wk), :], qrt) + winb_ref[jnp.minimum(it, nwin)]
    ew = jnp.exp2(s - jnp.max(s, axis=0, keepdims=True))
    ow = _dot(vwt_ref[0, 0, :, pl.ds(k0, wk)], ew.astype(BF16))
    o_win = ow[:HEAD_DIM] * (1.0 / ow[HEAD_DIM:HEAD_DIM + 1])

    ncp = kc_ref.shape[3]
    sc = _dot(kc_ref[0, 0, 0], qt)
    blk_end = lax.broadcasted_iota(I32, (ncp, 1), 0) * CMP_STRIDE + (CMP_BLOCK - 1)
    p = _softmax2_cols(jnp.where(blk_end <= qpos, sc, NEG))
    p = p * (qpos >= CMP_BLOCK - 1).astype(F32)
    o_cmp = _dot(vct_ref[0, 0, 0], p.astype(BF16))

    psum = p[:, 0:tq]
    for hh in range(1, HPG):
        psum = psum + p[:, hh * tq:(hh + 1) * tq]
    p_hi, p_lo = _split(psum)
    ovt = ovt_ref[...].astype(BF16)
    imp = _dot(ovt, p_hi) + _dot(ovt, p_lo)
    jj = lax.broadcasted_iota(I32, (nblk, 1), 0)
    qp1 = t0 + lax.broadcasted_iota(I32, (1, tq), 1)
    cur = qp1 // SEL_BLOCK
    forced = jnp.logical_or(jj == 0, jnp.logical_or(jj == cur, jj == cur - 1))
    score = jnp.where(forced, BIG, jnp.where(jj * SEL_BLOCK <= qp1, imp, NEG))
    ahead = jnp.zeros((nblk, tq), F32)
    for jp in range(nblk):
        sj = score[jp:jp + 1, :]
        first = jnp.logical_or(sj > score, jnp.logical_and(sj == score, jp < jj))
        ahead = ahead + first.astype(F32)
    selb = jnp.where(ahead < float(min(N_SEL, nblk)), 0.0, NEG)
    selb = jnp.concatenate([selb] * HPG, axis=1)
    pad = jnp.zeros((LANES - HEAD_DIM - nblk, nq), F32)
    qx = jnp.concatenate([qrt.astype(F32), selb, pad], axis=0).astype(BF16)

    kc = SLC_KC
    n = t0 // kc + 1
    kd = pl.multiple_of((n - 1) * kc, kc)
    sbufs = (s0_ref, s1_ref)
    ebufs = (e0_ref, e1_ref)
    s0_ref[...] = _dot(ks_ref[0, 0, pl.ds(kd, kc), :], qx) + diagb_ref[(t0 - kd) // tq]
    e1_ref[...] = jnp.zeros((kc, nq), BF16)

    def chunk_start(c):
        return pl.multiple_of(jnp.where(c <= 0, kd, (c - 1) * kc), kc)

    def pv(c, e_ref):
        return _dot(vst_ref[0, 0, :, pl.ds(chunk_start(c), kc)], e_ref[...])

    def step(i, par, carry):
        m, acc, alpha_p = carry
        acc = alpha_p * acc + pv(i - 2, ebufs[par])
        sp = sbufs[1 - par][...]
        m_new = jnp.maximum(m, jnp.max(sp, axis=0, keepdims=True))
        alpha = jnp.exp2(m - m_new)
        ebufs[1 - par][...] = jnp.exp2(sp - m_new).astype(BF16)
        k0 = chunk_start(jnp.minimum(i, n - 1))
        sbufs[par][...] = _dot(ks_ref[0, 0, pl.ds(k0, kc), :], qx)
        return m_new, acc, alpha

    def pair(p, carry):
        return step(2 * p + 2, 0, step(2 * p + 1, 1, carry))

    init = (jnp.full((1, nq), NEG, F32), jnp.zeros((VT_ROWS, nq), F32), jnp.ones((1, nq), F32))
    carry = lax.fori_loop(0, n // 2, pair, init)
    odd = n % 2 == 1
    _, acc, alpha_p = lax.cond(odd, lambda c: step(n, 1, c), lambda c: c, carry)
    acc = alpha_p * acc + jnp.where(odd, pv(n - 1, e0_ref), pv(n - 1, e1_ref))
    o_slc = acc[:HEAD_DIM] * (1.0 / acc[HEAD_DIM:HEAD_DIM + 1])

    gtt = gt_ref[0].T

    def gate(k):
        return jnp.concatenate([gtt[h * 3 + k:h * 3 + k + 1, :] for h in range(HPG)], axis=1)

    o = gate(0) * o_cmp + gate(1) * o_slc + gate(2) * o_win
    o = jnp.concatenate([o[:, h * tq:(h + 1) * tq] for h in range(HPG)], axis=0)
    o_ref[0] = o.T.astype(BF16)


def _attn_masks(tq):
    nq = HPG * tq
    lq = (jnp.arange(nq) % tq)[None, :]
    kr = jnp.arange(SLC_KC)[:, None]
    diag = jnp.stack([jnp.where(kr <= d * tq + lq, 0.0, NEG) for d in range(max(SLC_KC // tq, 1))])
    kr = jnp.arange(WINDOW + tq)[:, None]
    win = []
    for w in range(WINDOW // tq + 1):
        delta = w * tq + lq - kr
        win.append(jnp.where((delta >= 0) & (delta < WINDOW), 0.0, NEG))
    return diag.astype(F32), jnp.stack(win).astype(F32)


def _nsa_attention(q, qr, gates, cmp, cmp_t, ks, vst, kw, vwt, ovt, masks):
    bsz, seq, _ = q.shape
    ng, dh = kw.shape[1], kw.shape[3]
    tq = min(ATT_TQ, seq)
    nq = HPG * tq
    nblk = seq // SEL_BLOCK
    ncp = cmp.shape[3]
    diagb, winb = masks
    qspec = pl.BlockSpec((1, tq, HPG * dh), lambda b, g, i: (b, i, g))
    kern = functools.partial(_attn_kernel, tq=tq, nblk=nblk)
    return pl.pallas_call(
        kern,
        grid=(bsz, ng, seq // tq),
        in_specs=[
            qspec, qspec,
            pl.BlockSpec((1, tq, LANES), lambda b, g, i: (b, i, g)),
            pl.BlockSpec((1, 1, 1, ncp, dh), lambda b, g, i: (0, b, g, 0, 0)),
            pl.BlockSpec((1, 1, 1, dh, ncp), lambda b, g, i: (1, b, g, 0, 0)),
            pl.BlockSpec((1, 1, seq, LANES), lambda b, g, i: (b, g, 0, 0)),
            pl.BlockSpec((1, 1, VT_ROWS, seq), lambda b, g, i: (b, g, 0, 0)),
            pl.BlockSpec((1, 1, seq, dh), lambda b, g, i: (b, g, 0, 0)),
            pl.BlockSpec((1, 1, VT_ROWS, seq), lambda b, g, i: (b, g, 0, 0)),
            pl.BlockSpec((nblk, ncp), lambda b, g, i: (0, 0)),
            pl.BlockSpec(diagb.shape, lambda b, g, i: (0, 0, 0)),
            pl.BlockSpec(winb.shape, lambda b, g, i: (0, 0, 0)),
        ],
        out_specs=qspec,
        out_shape=jax.ShapeDtypeStruct((bsz, seq, ng * HPG * dh), BF16),
        scratch_shapes=[pltpu.VMEM((SLC_KC, nq), F32), pltpu.VMEM((SLC_KC, nq), F32),
                        pltpu.VMEM((SLC_KC, nq), BF16), pltpu.VMEM((SLC_KC, nq), BF16)],
        compiler_params=_cparams(("parallel", "parallel", "arbitrary")),
        name="nsa_attention",
    )(q, qr, gates, cmp, cmp_t, ks, vst, kw, vwt, ovt, diagb, winb)


def _overlap_t(seq):
    ncp = seq // CMP_STRIDE
    nblk = seq // SEL_BLOCK
    c_start = jnp.arange(ncp) * CMP_STRIDE
    j_start = jnp.arange(nblk) * SEL_BLOCK
    ov = (c_start[None, :] < j_start[:, None] + SEL_BLOCK) & (c_start[None, :] + CMP_BLOCK > j_start[:, None])
    return ov.astype(F32)


def _shared_kv(h3, w_kv, cmp_pos, cmp_w1, cmp_w2, tabs):
    raw, ks, vst, kw, vwt = _kv_proj(h3, w_kv, tabs)
    cmp, cmp_t = _compress(raw, cmp_w1, cmp_pos, cmp_w2)
    return cmp, cmp_t, ks, vst, kw, vwt


def _nsa_layer(h2, w_qg, w_o, shared, tabs, ovt, masks, g, b, bsz, seq):
    q, qr, gates = _qg_proj(h2.reshape(bsz, seq, -1), w_qg, tabs)
    o = _nsa_attention(q, qr, gates, *shared, ovt, masks)
    return _mm_postnorm(o.reshape(bsz * seq, -1), w_o, h2, g, b)


def kernel(x, ln_g, ln_b, lru_w_in, lru_conv_w, lru_conv_b, lru_w_a, lru_b_a, lru_w_i, lru_b_i, lru_lambda,
           lru_w_out, nsa_w_kv, nsa_cmp_pos, nsa_cmp_w1, nsa_cmp_w2, nsa_w_qg, nsa_w_o, ffn_w_gu, ffn_w_down,
           moe_w_router, moe_w_gu, moe_w_down):
    bsz, seq, d = x.shape
    tabs = _rope_tables(seq)
    ovt = _overlap_t(seq)
    masks = _attn_masks(min(ATT_TQ, seq))
    h = x
    shared = None
    for l in range(DEPTH):
        if l < N_A_LAYERS:
            h = _lru_layer(h, lru_w_in[l], lru_conv_w[l], lru_conv_b[l], lru_w_a[l], lru_b_a[l].reshape(-1),
                           lru_w_i[l], lru_b_i[l].reshape(-1), lru_lambda[l], lru_w_out[l],
                           ln_g[l, 0], ln_b[l, 0])
            h2 = h.reshape(bsz * seq, d)
        else:
            lb = l - N_A_LAYERS
            h2 = _nsa_layer(h2, nsa_w_qg[lb], nsa_w_o[lb], shared, tabs, ovt, masks, ln_g[l, 0], ln_b[l, 0],
                            bsz, seq)
        if l % 2 == 0:
            h2 = _dense_ffn(h2, ffn_w_gu, ffn_w_down, l // 2, ln_g[l, 1], ln_b[l, 1])
        else:
            h2 = _moe(h2, moe_w_router[l // 2], moe_w_gu, moe_w_down, l // 2, ln_g[l, 1], ln_b[l, 1])
        h = h2.reshape(bsz, seq, d)
        if l == N_A_LAYERS - 1:
            shared = _shared_kv(h, nsa_w_kv, nsa_cmp_pos, nsa_cmp_w1, nsa_cmp_w2, tabs)
    return h
```

```python
import functools

import jax
import jax.numpy as jnp
from jax import lax
from jax.experimental import pallas as pl
from jax.experimental.pallas import tpu as pltpu

F32 = jnp.float32
BF16 = jnp.bfloat16
I32 = jnp.int32

DEPTH = 4
N_A_LAYERS = DEPTH // 2
LRU_BLOCKS = 8
CONV_WIDTH = 4
LRU_C = 8.0
N_HEADS = 16
N_KV = 4
HPG = N_HEADS // N_KV
HEAD_DIM = 64
ROT_DIM = HEAD_DIM // 4
ROPE_THETA = 500000.0
CMP_BLOCK = 32
CMP_STRIDE = 16
SEL_BLOCK = 64
N_SEL = 8
WINDOW = 256
N_EXPERTS = 8
DN_ALPHA = (2.0 * DEPTH) ** 0.25
LN_EPS = 1e-5
NEG = -1e30
BIG = 1e30
LOG2E = 1.4426950408889634

LANES = 128
SUBLANES = 8
VMEM_LIMIT = 52 * 1024 * 1024

LRU_TS = 256
FFN_TM = 1024
DENSE_TM = 512
DENSE_TF = 1408
ROUTE_TM = 512
PROJ_TM = 512
ATT_TQ = 256
SLC_KC = 256
SEG_ALIGN = 16
VT_ROWS = 80


def _cparams(sem):
    return pltpu.CompilerParams(dimension_semantics=sem, vmem_limit_bytes=VMEM_LIMIT)


def _dot(a, b):
    return jnp.dot(a, b, preferred_element_type=F32)


def _split(a):
    hi = a.astype(BF16)
    lo = (a - hi.astype(F32)).astype(BF16)
    return hi, lo


def _dot_f32(a, b):
    a_hi, a_lo = _split(a)
    b_hi, b_lo = _split(b)
    return _dot(a_hi, b_hi) + (_dot(a_hi, b_lo) + _dot(a_lo, b_hi))


def _sigmoid(x):
    return 0.5 * jnp.tanh(0.5 * x) + 0.5


def _postnorm(res, y, g, b):
    z = DN_ALPHA * res + y
    mu = jnp.mean(z, axis=-1, keepdims=True)
    zc = z - mu
    var = jnp.mean(zc * zc, axis=-1, keepdims=True)
    return zc * lax.rsqrt(var + LN_EPS) * g + b


def _lru_kernel(x_ref, win_ref, cw_ref, cb_ref, wai_ref, ba_ref, bi_ref, lam_ref, wout_ref,
                g_ref, b_ref, o_ref, hcar_ref, xprev_ref, hg_ref, *, ts, width, nblk):
    bw = width // nblk

    @pl.when(pl.program_id(1) == 0)
    def _():
        hcar_ref[...] = jnp.zeros_like(hcar_ref)
        xprev_ref[...] = jnp.zeros_like(xprev_ref)

    x = x_ref[0]
    u = _dot(x.astype(BF16), win_ref[...])
    xc = u[:, width:]
    xcat = jnp.concatenate([xprev_ref[...], xc], axis=0)
    xr = cb_ref[...] + cw_ref[CONV_WIDTH - 1:CONV_WIDTH, :] * xc
    for s in range(1, CONV_WIDTH):
        xr = xr + cw_ref[CONV_WIDTH - 1 - s:CONV_WIDTH - s, :] * xcat[SUBLANES - s:SUBLANES - s + ts]
    xprev_ref[...] = xc[ts - SUBLANES:ts]

    row = lax.broadcasted_iota(I32, (ts // SUBLANES, SUBLANES, bw), 1)
    for n in range(nblk):
        sl = slice(n * bw, (n + 1) * bw)
        xb = xr[:, sl]
        ri = _dot(xb.astype(BF16), wai_ref[n])
        r = _sigmoid(ri[:, :bw] + ba_ref[:, sl])
        i = _sigmoid(ri[:, bw:] + bi_ref[:, sl])
        lam = lam_ref[:, sl]
        sp = jnp.maximum(-lam, 0.0) + jnp.log(1.0 + jnp.exp(-jnp.abs(lam)))
        log_a = (-LRU_C * r) * sp
        a = jnp.exp(log_a)
        th = jnp.tanh(log_a)
        h = jnp.sqrt(-2.0 * th / (1.0 - th)) * (i * xb)
        a = a.reshape(ts // SUBLANES, SUBLANES, bw)
        h = h.reshape(ts // SUBLANES, SUBLANES, bw)
        d = 1
        while d < SUBLANES:
            keep = row >= d
            a_sh = jnp.where(keep, pltpu.roll(a, d, 1), 1.0)
            h_sh = jnp.where(keep, pltpu.roll(h, d, 1), 0.0)
            h = a * h_sh + h
            a = a * a_sh
            d *= 2
        carry = hcar_ref[:, sl]
        groups = []
        for v in range(ts // SUBLANES):
            hv = h[v] + a[v] * carry
            carry = hv[SUBLANES - 1:SUBLANES, :]
            groups.append(hv)
        h = jnp.concatenate(groups, axis=0)
        hcar_ref[:, sl] = carry
        gate = jax.nn.gelu(u[:, sl])
        hg_ref[:, sl] = (h * gate).astype(BF16)

    y = _dot(hg_ref[...], wout_ref[...])
    o_ref[0] = _postnorm(x, y, g_ref[...], b_ref[...])


def _lru_layer(x, w_in, conv_w, conv_b, w_a, b_a, w_i, b_i, lam, w_out, g, b):
    bsz, seq, d = x.shape
    width = w_out.shape[0]
    nblk = w_a.shape[0]
    ts = min(LRU_TS, seq)
    wai = jnp.concatenate([w_a, w_i], axis=-1).astype(BF16)
    row2 = lambda v: v.reshape(1, -1)
    const2 = lambda bi, ti: (0, 0)
    kern = functools.partial(_lru_kernel, ts=ts, width=width, nblk=nblk)
    return pl.pallas_call(
        kern,
        grid=(bsz, seq // ts),
        in_specs=[
            pl.BlockSpec((1, ts, d), lambda bi, ti: (bi, ti, 0)),
            pl.BlockSpec((d, 2 * width), const2),
            pl.BlockSpec((CONV_WIDTH, width), const2),
            pl.BlockSpec((1, width), const2),
            pl.BlockSpec(wai.shape, lambda bi, ti: (0, 0, 0)),
            pl.BlockSpec((1, width), const2),
            pl.BlockSpec((1, width), const2),
            pl.BlockSpec((1, width), const2),
            pl.BlockSpec((width, d), const2),
            pl.BlockSpec((1, d), const2),
            pl.BlockSpec((1, d), const2),
        ],
        out_specs=pl.BlockSpec((1, ts, d), lambda bi, ti: (bi, ti, 0)),
        out_shape=jax.ShapeDtypeStruct((bsz, seq, d), F32),
        scratch_shapes=[
            pltpu.VMEM((1, width), F32),
            pltpu.VMEM((SUBLANES, width), F32),
            pltpu.VMEM((ts, width), BF16),
        ],
        compiler_params=_cparams(("parallel", "arbitrary")),
        name="lru_layer",
    )(x, w_in.astype(BF16), conv_w, row2(conv_b), wai, row2(b_a), row2(b_i), row2(lam),
      w_out.astype(BF16), row2(g), row2(b))


def _ffn_kernel(te_ref, nu_ref, x_ref, wg_ref, wu_ref, wd_ref, g_ref, b_ref, o_ref, xb_ref, acc_ref,
                *, nf, postnorm):
    i = pl.program_id(0)
    f = pl.program_id(1)
    used = i < nu_ref[0]

    @pl.when(used)
    def _():
        @pl.when(f == 0)
        def _():
            xb_ref[...] = x_ref[...].astype(BF16)
            acc_ref[...] = jnp.zeros_like(acc_ref)

        xb = xb_ref[...]
        gp = _dot(xb, wg_ref[0, 0].astype(BF16))
        up = _dot(xb, wu_ref[0, 0].astype(BF16))
        act = (gp * jax.nn.sigmoid(gp) * up).astype(BF16)
        acc_ref[...] += _dot(act, wd_ref[0, 0].astype(BF16))

        @pl.when(f == nf - 1)
        def _():
            if postnorm:
                o_ref[...] = _postnorm(x_ref[...], acc_ref[...], g_ref[...], b_ref[...])
            else:
                o_ref[...] = acc_ref[...].astype(o_ref.dtype)

    @pl.when(jnp.logical_and(jnp.logical_not(used), f == nf - 1))
    def _():
        o_ref[...] = jnp.zeros_like(o_ref)


def _ffn(x, w_gu, w_down, layer, tile_expert, n_used, g, b, *, postnorm, tf, tm):
    rows, d = x.shape
    ff = w_down.shape[2]
    nf = ff // tf
    nt = rows // tm

    def fidx(i, f, nu):
        return jnp.where(i < nu[0], f, nf - 1)

    kern = functools.partial(_ffn_kernel, nf=nf, postnorm=postnorm)
    grid_spec = pltpu.PrefetchScalarGridSpec(
        num_scalar_prefetch=2,
        grid=(nt, nf),
        in_specs=[
            pl.BlockSpec((tm, d), lambda i, f, te, nu: (i, 0)),
            pl.BlockSpec((1, 1, d, tf), lambda i, f, te, nu: (layer, te[i], 0, fidx(i, f, nu))),
            pl.BlockSpec((1, 1, d, tf), lambda i, f, te, nu: (layer, te[i], 0, fidx(i, f, nu) + nf)),
            pl.BlockSpec((1, 1, tf, d), lambda i, f, te, nu: (layer, te[i], fidx(i, f, nu), 0)),
            pl.BlockSpec((1, d), lambda i, f, te, nu: (0, 0)),
            pl.BlockSpec((1, d), lambda i, f, te, nu: (0, 0)),
        ],
        out_specs=pl.BlockSpec((tm, d), lambda i, f, te, nu: (i, 0)),
        scratch_shapes=[pltpu.VMEM((tm, d), BF16), pltpu.VMEM((tm, d), F32)],
    )
    return pl.pallas_call(
        kern,
        grid_spec=grid_spec,
        out_shape=jax.ShapeDtypeStruct((rows, d), F32 if postnorm else BF16),
        compiler_params=_cparams(("parallel", "arbitrary")),
        name="ffn_postnorm" if postnorm else "ffn_experts",
    )(tile_expert, n_used, x, w_gu, w_gu, w_down, g.reshape(1, -1), b.reshape(1, -1))


def _dense_ffn(h2, w_gu, w_down, layer, g, b):
    rows = h2.shape[0]
    tm = min(DENSE_TM, rows)
    nt = rows // tm
    ff = w_down.shape[1]
    tf = DENSE_TF if ff % DENSE_TF == 0 else LANES
    return _ffn(h2, w_gu[:, None].astype(BF16), w_down[:, None].astype(BF16), layer,
                jnp.zeros((nt,), I32), jnp.full((1,), nt, I32), g, b, postnorm=True, tf=tf, tm=tm)


def _router_kernel(h_ref, wr_ref, ut_ref, xl_ref, mi_ref, mw_ref, seg_ref, *, tm, n_exp, loc):
    x = h_ref[...]
    logits = _dot_f32(x, wr_ref[...])
    lane = lax.broadcasted_iota(I32, (tm, LANES), 1)
    lg = jnp.where(lane < n_exp, logits, -jnp.inf)
    m0 = jnp.max(lg, axis=1, keepdims=True)
    i0 = jnp.min(jnp.where(lg == m0, lane, LANES), axis=1, keepdims=True)
    lg1 = jnp.where(lane == i0, -jnp.inf, lg)
    m1 = jnp.max(lg1, axis=1, keepdims=True)
    i1 = jnp.min(jnp.where(lg1 == m1, lane, LANES), axis=1, keepdims=True)
    e1 = jnp.exp(m1 - m0)
    w0 = 1.0 / (1.0 + e1)
    w1 = e1 / (1.0 + e1)

    oh0 = lane == i0
    oh1 = lane == i1
    oh = jnp.logical_or(oh0, oh1).astype(F32)
    tri = (lax.broadcasted_iota(I32, (tm, tm), 0) > lax.broadcasted_iota(I32, (tm, tm), 1)).astype(BF16)
    before = _dot(tri, oh.astype(BF16))
    cnt = jnp.sum(oh, axis=0, keepdims=True)
    padded = jnp.floor((cnt + (SEG_ALIGN - 1)) * (1.0 / SEG_ALIGN)) * SEG_ALIGN
    seg = _dot(jnp.broadcast_to(padded, (SUBLANES, LANES)).astype(BF16), ut_ref[...].astype(BF16))[0:1, :]
    lp0 = jnp.sum(jnp.where(oh0, seg + before, 0.0), axis=1, keepdims=True)
    lp1 = jnp.sum(jnp.where(oh1, seg + before, 0.0), axis=1, keepdims=True)
    lpt = jnp.where(lane == 0, lp0, jnp.where(lane == 1, lp1, -1.0)).T
    prow = lax.broadcasted_iota(I32, (loc, tm), 0).astype(F32)
    place = jnp.logical_or(prow == lpt[0:1, :], prow == lpt[1:2, :]).astype(BF16)
    xl_ref[0] = _dot(place, x.astype(BF16)).astype(BF16)

    mi_ref[...] = jnp.where(lane == 0, lp0, lp1).astype(I32)
    mw_ref[...] = jnp.where(lane == 0, w0, w1)
    row8 = lax.broadcasted_iota(I32, (SUBLANES, LANES), 0)
    seg_ref[0] = jnp.where(row8 == 0, cnt, seg).astype(I32)


def _router(h2, w_router):
    rows, d = h2.shape
    n_exp = w_router.shape[1]
    tm = min(ROUTE_TM, rows)
    nt = rows // tm
    loc = 2 * tm + LANES
    assert n_exp * (SEG_ALIGN - 1) <= LANES
    wr = jnp.pad(w_router, ((0, 0), (0, LANES - n_exp)))
    upper = (jnp.arange(LANES)[:, None] < jnp.arange(LANES)[None, :]).astype(F32)
    kern = functools.partial(_router_kernel, tm=tm, n_exp=n_exp, loc=loc)
    return pl.pallas_call(
        kern,
        grid=(nt,),
        in_specs=[pl.BlockSpec((tm, d), lambda i: (i, 0)), pl.BlockSpec((d, LANES), lambda i: (0, 0)),
                  pl.BlockSpec((LANES, LANES), lambda i: (0, 0))],
        out_specs=[
            pl.BlockSpec((1, loc, d), lambda i: (i, 0, 0)),
            pl.BlockSpec((tm, LANES), lambda i: (i, 0)),
            pl.BlockSpec((tm, LANES), lambda i: (i, 0)),
            pl.BlockSpec((1, SUBLANES, LANES), lambda i: (i, 0, 0)),
        ],
        out_shape=[
            jax.ShapeDtypeStruct((nt, loc, d), BF16),
            jax.ShapeDtypeStruct((rows, LANES), I32),
            jax.ShapeDtypeStruct((rows, LANES), F32),
            jax.ShapeDtypeStruct((nt, SUBLANES, LANES), I32),
        ],
        compiler_params=_cparams(("parallel",)),
        name="moe_router",
    )(h2, wr, upper)


def _segment_copies(loc_ref, srt_ref, units_ref, local_buf, sorted_hbm, sem, tile, n_exp, nbits, to_sorted, do):
    base = tile * n_exp
    for e in range(n_exp):
        lo = loc_ref[base + e]
        so = srt_ref[base + e]
        units = units_ref[base + e]
        off = 0
        for k in reversed(range(nbits)):
            size = SEG_ALIGN << k
            bit = (units >> k) & 1

            @pl.when(bit == 1)
            def _(off=off, size=size, lo=lo, so=so):
                local = local_buf.at[pl.ds(pl.multiple_of(lo + off, SEG_ALIGN), size)]
                srt = sorted_hbm.at[pl.ds(pl.multiple_of(so + off, SEG_ALIGN), size)]
                do(pltpu.make_async_copy(local, srt, sem) if to_sorted else pltpu.make_async_copy(srt, local, sem))

            off = off + bit * size


def _spread_kernel(loc_ref, srt_ref, units_ref, xl_ref, zin_ref, o_ref, sem, *, n_exp, nbits):
    del zin_ref
    for do in (lambda cp: cp.start(), lambda cp: cp.wait()):
        _segment_copies(loc_ref, srt_ref, units_ref, xl_ref.at[0], o_ref, sem, pl.program_id(0), n_exp, nbits,
                        True, do)


def _spread_segments(x_loc, seg_start, sorted_start, units, n_out, nbits):
    nt, loc, d = x_loc.shape
    n_exp = units.shape[1]
    kern = functools.partial(_spread_kernel, n_exp=n_exp, nbits=nbits)
    grid_spec = pltpu.PrefetchScalarGridSpec(
        num_scalar_prefetch=3,
        grid=(nt,),
        in_specs=[pl.BlockSpec((1, loc, d), lambda i, a, b, c: (i, 0, 0)), pl.BlockSpec(memory_space=pl.ANY)],
        out_specs=pl.BlockSpec(memory_space=pl.ANY),
        scratch_shapes=[pltpu.SemaphoreType.DMA(())],
    )
    return pl.pallas_call(
        kern,
        grid_spec=grid_spec,
        out_shape=jax.ShapeDtypeStruct((n_out, d), x_loc.dtype),
        input_output_aliases={4: 0},
        compiler_params=_cparams(("arbitrary",)),
        name="moe_spread",
    )(seg_start.reshape(-1), sorted_start.reshape(-1), units.reshape(-1), x_loc, jnp.zeros((n_out, d), x_loc.dtype))


def _combine_kernel(loc_ref, srt_ref, units_ref, mi_ref, mw_ref, h_ref, ys_ref, g_ref, b_ref, o_ref, yl_ref, sem,
                    *, tm, n_exp, nbits):
    loc = yl_ref.shape[1]
    i = pl.program_id(0)
    slot = i % 2

    def copies(tile, buf, do):
        _segment_copies(loc_ref, srt_ref, units_ref, yl_ref.at[buf], ys_ref, sem.at[buf], tile, n_exp, nbits,
                        False, do)

    @pl.when(i == 0)
    def _():
        yl_ref[...] = jnp.zeros_like(yl_ref)
        copies(0, 0, lambda cp: cp.start())

    @pl.when(i + 1 < pl.num_programs(0))
    def _():
        copies(i + 1, 1 - slot, lambda cp: cp.start())

    copies(i, slot, lambda cp: cp.wait())
    mi = mi_ref[...]
    w = mw_ref[...]
    col = lax.broadcasted_iota(I32, (tm, loc), 1)
    wmat = jnp.where(col == mi[:, 0:1], w[:, 0:1], 0.0) + jnp.where(col == mi[:, 1:2], w[:, 1:2], 0.0)
    w_hi, w_lo = _split(wmat)
    yl = yl_ref[slot]
    y = _dot(w_hi, yl) + _dot(w_lo, yl)
    o_ref[...] = _postnorm(h_ref[...], y, g_ref[...], b_ref[...])


def _combine(h2, ys, seg_start, sorted_start, units, loc, mi, mw, g, b, nbits):
    rows, d = h2.shape
    nt, n_exp = units.shape
    tm = rows // nt
    kern = functools.partial(_combine_kernel, tm=tm, n_exp=n_exp, nbits=nbits)
    grid_spec = pltpu.PrefetchScalarGridSpec(
        num_scalar_prefetch=3,
        grid=(nt,),
        in_specs=[
            pl.BlockSpec((tm, LANES), lambda i, a, b, c: (i, 0)),
            pl.BlockSpec((tm, LANES), lambda i, a, b, c: (i, 0)),
            pl.BlockSpec((tm, d), lambda i, a, b, c: (i, 0)),
            pl.BlockSpec(memory_space=pl.ANY),
            pl.BlockSpec((1, d), lambda i, a, b, c: (0, 0)),
            pl.BlockSpec((1, d), lambda i, a, b, c: (0, 0)),
        ],
        out_specs=pl.BlockSpec((tm, d), lambda i, a, b, c: (i, 0)),
        scratch_shapes=[pltpu.VMEM((2, loc, d), ys.dtype), pltpu.SemaphoreType.DMA((2,))],
    )
    return pl.pallas_call(
        kern,
        grid_spec=grid_spec,
        out_shape=jax.ShapeDtypeStruct((rows, d), F32),
        compiler_params=_cparams(("arbitrary",)),
        name="moe_combine",
    )(seg_start.reshape(-1), sorted_start.reshape(-1), units.reshape(-1), mi, mw, h2, ys,
      g.reshape(1, -1), b.reshape(1, -1))


def _moe(h2, w_router, w_gu, w_down, layer, g, b):
    rows, d = h2.shape
    n_exp = w_router.shape[1]
    tm = min(FFN_TM, rows)
    x_loc, mi, mw, seg = _router(h2, w_router)
    nt_r, loc, _ = x_loc.shape
    counts = seg[:, 0, :n_exp]
    seg_start = seg[:, 1, :n_exp]
    units = (counts + SEG_ALIGN - 1) // SEG_ALIGN
    seg_len = units * SEG_ALIGN
    group_rows = jnp.sum(seg_len, axis=0)
    tiles_per = (group_rows + tm - 1) // tm
    tile_end = jnp.cumsum(tiles_per)
    group_start = (tile_end - tiles_per) * tm
    sorted_start = group_start[None, :] + jnp.cumsum(seg_len, axis=0) - seg_len
    n_tiles = -(-(2 * rows + nt_r * n_exp * (SEG_ALIGN - 1)) // tm) + n_exp
    n_used = tile_end[-1]
    tid = jnp.minimum(jnp.arange(n_tiles, dtype=I32), n_used - 1)
    tile_expert = jnp.sum((tid[:, None] >= tile_end[None, :]).astype(I32), axis=1)
    tile_expert = jnp.minimum(tile_expert, n_exp - 1)
    nbits = (min(ROUTE_TM, rows) // SEG_ALIGN).bit_length()

    xs = _spread_segments(x_loc, seg_start, sorted_start, units, n_tiles * tm, nbits)
    ys = _ffn(xs, w_gu, w_down, layer, tile_expert, n_used.reshape(1), g, b, postnorm=False, tf=512, tm=tm)
    return _combine(h2, ys, seg_start, sorted_start, units, loc, mi, mw, g, b, nbits)


def _rope_cols(x, c, sa, sb):
    half = ROT_DIM // 2
    return x * c + pltpu.roll(x, LANES - half, 1) * sa + pltpu.roll(x, half, 1) * sb


def _rope_tables(seq):
    half = ROT_DIM // 2
    inv = ROPE_THETA ** (-jnp.arange(half, dtype=F32) / half)
    ang = jnp.arange(seq, dtype=F32)[:, None] * inv[None]
    cos, sin = jnp.cos(ang), jnp.sin(ang)
    ones = jnp.ones((seq, HEAD_DIM - ROT_DIM), F32)
    zeros = jnp.zeros((seq, HEAD_DIM - ROT_DIM), F32)
    zh = jnp.zeros((seq, half), F32)
    c = jnp.concatenate([cos, cos, ones], axis=1)
    sa = jnp.concatenate([-sin, zh, zeros], axis=1)
    sb = jnp.concatenate([zh, sin, zeros], axis=1)
    rep = LANES // HEAD_DIM
    return tuple(jnp.tile(t, (1, rep)) for t in (c, sa, sb))


def _kv_kernel(h_ref, w_ref, c_ref, sa_ref, sb_ref, raw_ref, ks_ref, vst_ref, kw_ref, vwt_ref, *, gw):
    res = _dot(h_ref[0].astype(BF16), w_ref[...])
    c, sa, sb = c_ref[...], sa_ref[...], sb_ref[...]
    ng = gw // HEAD_DIM

    def part(p):
        return res[:, p * gw:(p + 1) * gw]

    def roped(p):
        x = part(p)
        return jnp.concatenate(
            [_rope_cols(x[:, j * LANES:(j + 1) * LANES], c, sa, sb) for j in range(gw // LANES)], axis=1)

    for t in range(2):
        x = part(t)
        for g in range(ng):
            raw_ref[t, 0, g] = x[:, g * HEAD_DIM:(g + 1) * HEAD_DIM]
    tm = res.shape[0]
    pos = pl.program_id(1) * tm + lax.broadcasted_iota(I32, (tm, LANES - HEAD_DIM), 0)
    blk_onehot = (pos // SEL_BLOCK == lax.broadcasted_iota(I32, (tm, LANES - HEAD_DIM), 1)).astype(F32)
    ones_rows = (lax.broadcasted_iota(I32, (VT_ROWS - HEAD_DIM, tm), 0) == 0).astype(F32)
    for p, k_ref, vt_ref in ((2, ks_ref, vst_ref), (4, kw_ref, vwt_ref)):
        k = roped(p)
        vt = part(p + 1).T
        for g in range(ng):
            kg = k[:, g * HEAD_DIM:(g + 1) * HEAD_DIM]
            if p == 2:
                kg = jnp.concatenate([kg, blk_onehot], axis=1)
            k_ref[0, g] = kg.astype(BF16)
            vt_ref[0, g] = jnp.concatenate([vt[g * HEAD_DIM:(g + 1) * HEAD_DIM, :], ones_rows], axis=0).astype(BF16)


def _kv_proj(h3, w_kv, tabs):
    bsz, seq, d = h3.shape
    n = w_kv.shape[1]
    gw = n // 6
    ng = gw // HEAD_DIM
    tm = min(PROJ_TM, seq)
    assert seq // SEL_BLOCK <= LANES - HEAD_DIM
    tab_spec = pl.BlockSpec((tm, LANES), lambda b, i: (i, 0))

    def k_spec(w):
        return pl.BlockSpec((1, ng, tm, w), lambda b, i: (b, 0, i, 0))

    def k_shape(w):
        return jax.ShapeDtypeStruct((bsz, ng, seq, w), BF16)

    vt_spec = pl.BlockSpec((1, ng, VT_ROWS, tm), lambda b, i: (b, 0, 0, i))
    vt_shape = jax.ShapeDtypeStruct((bsz, ng, VT_ROWS, seq), BF16)
    kern = functools.partial(_kv_kernel, gw=gw)
    return pl.pallas_call(
        kern,
        grid=(bsz, seq // tm),
        in_specs=[pl.BlockSpec((1, tm, d), lambda b, i: (b, i, 0)), pl.BlockSpec((d, n), lambda b, i: (0, 0)),
                  tab_spec, tab_spec, tab_spec],
        out_specs=[pl.BlockSpec((2, 1, ng, tm, HEAD_DIM), lambda b, i: (0, b, 0, i, 0)),
                   k_spec(LANES), vt_spec, k_spec(HEAD_DIM), vt_spec],
        out_shape=[jax.ShapeDtypeStruct((2, bsz, ng, seq, HEAD_DIM), F32),
                   k_shape(LANES), vt_shape, k_shape(HEAD_DIM), vt_shape],
        compiler_params=_cparams(("parallel", "parallel")),
        name="kv_proj",
    )(h3, w_kv.astype(BF16), *tabs)


def _qg_kernel(h_ref, w_ref, c_ref, sa_ref, sb_ref, q_ref, qr_ref, gt_ref, *, nq):
    res = _dot(h_ref[0].astype(BF16), w_ref[...])
    c, sa, sb = c_ref[...], sa_ref[...], sb_ref[...]
    scale = HEAD_DIM ** -0.5 * LOG2E
    for j in range(nq // LANES):
        x = res[:, j * LANES:(j + 1) * LANES] * scale
        q_ref[0, :, j * LANES:(j + 1) * LANES] = x.astype(BF16)
        qr_ref[0, :, j * LANES:(j + 1) * LANES] = _rope_cols(x, c, sa, sb).astype(BF16)
    gt_ref[0] = jax.nn.sigmoid(res[:, nq:])


def _qg_proj(h3, w_qg, tabs):
    bsz, seq, d = h3.shape
    nq = N_HEADS * HEAD_DIM
    ngate = HPG * 3
    tm = min(PROJ_TM, seq)
    wg = w_qg[:, nq:].reshape(d, N_KV, ngate)
    wg = jnp.pad(wg, ((0, 0), (0, 0), (0, LANES - ngate))).reshape(d, N_KV * LANES)
    w = jnp.concatenate([w_qg[:, :nq], wg], axis=1).astype(BF16)
    n = w.shape[1]
    tab_spec = pl.BlockSpec((tm, LANES), lambda b, i: (i, 0))
    kern = functools.partial(_qg_kernel, nq=nq)
    return pl.pallas_call(
        kern,
        grid=(bsz, seq // tm),
        in_specs=[pl.BlockSpec((1, tm, d), lambda b, i: (b, i, 0)), pl.BlockSpec((d, n), lambda b, i: (0, 0)),
                  tab_spec, tab_spec, tab_spec],
        out_specs=[pl.BlockSpec((1, tm, nq), lambda b, i: (b, i, 0)), pl.BlockSpec((1, tm, nq), lambda b, i: (b, i, 0)),
                   pl.BlockSpec((1, tm, N_KV * LANES), lambda b, i: (b, i, 0))],
        out_shape=[jax.ShapeDtypeStruct((bsz, seq, nq), BF16), jax.ShapeDtypeStruct((bsz, seq, nq), BF16),
                   jax.ShapeDtypeStruct((bsz, seq, N_KV * LANES), F32)],
        compiler_params=_cparams(("parallel", "parallel")),
        name="qg_proj",
    )(h3, w, *tabs)


def _mm_ln_kernel(a_ref, w_ref, res_ref, g_ref, b_ref, o_ref):
    y = _dot(a_ref[...], w_ref[...])
    o_ref[...] = _postnorm(res_ref[...], y, g_ref[...], b_ref[...])


def _mm_postnorm(a, w, res, g, b):
    rows, k = a.shape
    d = w.shape[1]
    tm = min(PROJ_TM, rows)
    return pl.pallas_call(
        _mm_ln_kernel,
        grid=(rows // tm,),
        in_specs=[pl.BlockSpec((tm, k), lambda i: (i, 0)), pl.BlockSpec((k, d), lambda i: (0, 0)),
                  pl.BlockSpec((tm, d), lambda i: (i, 0)),
                  pl.BlockSpec((1, d), lambda i: (0, 0)), pl.BlockSpec((1, d), lambda i: (0, 0))],
        out_specs=pl.BlockSpec((tm, d), lambda i: (i, 0)),
        out_shape=jax.ShapeDtypeStruct((rows, d), F32),
        compiler_params=_cparams(("parallel",)),
        name="out_proj_postnorm",
    )(a, w.astype(BF16), res, g.reshape(1, -1), b.reshape(1, -1))


def _compress_kernel(raw_ref, w1_ref, pe_ref, w2_ref, o_ref, ot_ref, *, rows_per):
    half = w1_ref.shape[1] // 2
    pieces = [raw_ref[0, 0, 0, pl.ds(l, rows_per, stride=CMP_STRIDE), :] for l in range(CMP_STRIDE)]
    x = jnp.concatenate(pieces, axis=1).astype(BF16)
    first = _dot(x, w1_ref[0, :half, :])
    second = _dot(x, w1_ref[0, half:, :])
    bias = _dot(pe_ref[0].astype(BF16), w1_ref[0])[0:1, :]
    hid = first + pltpu.roll(second, rows_per - 1, 0) + bias
    out = _dot(jax.nn.gelu(hid).astype(BF16), w2_ref[0])
    o_ref[0, 0, 0] = out.astype(BF16)
    wide = jnp.concatenate([out, jnp.zeros_like(out)], axis=1)
    ot_ref[0, 0, 0] = wide.T[:out.shape[1], :].astype(BF16)


def _compress(raw, w1, pe, w2):
    _, bsz, ng, seq, dh = raw.shape
    rows_per = seq // CMP_STRIDE
    k2 = w1.shape[1]
    hid = w1.shape[2]
    pe8 = jnp.broadcast_to(pe.reshape(2, 1, k2), (2, SUBLANES, k2))
    kern = functools.partial(_compress_kernel, rows_per=rows_per)
    return pl.pallas_call(
        kern,
        grid=(2, bsz, ng),
        in_specs=[
            pl.BlockSpec((1, 1, 1, seq, dh), lambda t, b, g: (t, b, g, 0, 0)),
            pl.BlockSpec((1, k2, hid), lambda t, b, g: (t, 0, 0)),
            pl.BlockSpec((1, SUBLANES, k2), lambda t, b, g: (t, 0, 0)),
            pl.BlockSpec((1, hid, dh), lambda t, b, g: (t, 0, 0)),
        ],
        out_specs=[pl.BlockSpec((1, 1, 1, rows_per, dh), lambda t, b, g: (t, b, g, 0, 0)),
                   pl.BlockSpec((1, 1, 1, dh, rows_per), lambda t, b, g: (t, b, g, 0, 0))],
        out_shape=[jax.ShapeDtypeStruct((2, bsz, ng, rows_per, dh), BF16),
                   jax.ShapeDtypeStruct((2, bsz, ng, dh, rows_per), BF16)],
        compiler_params=_cparams(("parallel", "parallel", "parallel")),
        name="kv_compress",
    )(raw, w1.astype(BF16), pe8, w2.astype(BF16))


def _softmax2_cols(s):
    m = jnp.max(s, axis=0, keepdims=True)
    e = jnp.exp2(s - m)
    return e * (1.0 / jnp.sum(e, axis=0, keepdims=True))


def _heads_t(x):
    xt = x.astype(F32).T
    return jnp.concatenate([xt[h * HEAD_DIM:(h + 1) * HEAD_DIM, :] for h in range(HPG)], axis=1)


def _attn_kernel(q_ref, qr_ref, gt_ref, kc_ref, vct_ref, ks_ref, vst_ref, kw_ref, vwt_ref, ovt_ref,
                 diagb_ref, winb_ref, o_ref, s0_ref, s1_ref, e0_ref, e1_ref, *, tq, nblk):
    nq = HPG * tq
    it = pl.program_id(2)
    t0 = it * tq
    lane_q = lax.broadcasted_iota(I32, (1, nq), 1)
    qpos = t0 + lane_q % tq
    qt = _heads_t(q_ref[0]).astype(BF16)
    qrt = _heads_t(qr_ref[0]).astype(BF16)

    wk = WINDOW + tq
    nwin = WINDOW // tq
    k0 = pl.multiple_of(jnp.maximum(t0 - WINDOW, 0), tq)
    s = _dot(kw_ref[0, 0, pl.ds(k0, wk), :], qrt) + winb_ref[jnp.minimum(it, nwin)]
    ew = jnp.exp2(s - jnp.max(s, axis=0, keepdims=True))
    ow = _dot(vwt_ref[0, 0, :, pl.ds(k0, wk)], ew.astype(BF16))
    o_win = ow[:HEAD_DIM] * (1.0 / ow[HEAD_DIM:HEAD_DIM + 1])

    ncp = kc_ref.shape[3]
    sc = _dot(kc_ref[0, 0, 0], qt)
    blk_end = lax.broadcasted_iota(I32, (ncp, 1), 0) * CMP_STRIDE + (CMP_BLOCK - 1)
    p = _softmax2_cols(jnp.where(blk_end <= qpos, sc, NEG))
    p = p * (qpos >= CMP_BLOCK - 1).astype(F32)
    o_cmp = _dot(vct_ref[0, 0, 0], p.astype(BF16))

    psum = p[:, 0:tq]
    for hh in range(1, HPG):
        psum = psum + p[:, hh * tq:(hh + 1) * tq]
    p_hi, p_lo = _split(psum)
    ovt = ovt_ref[...].astype(BF16)
    imp = _dot(ovt, p_hi) + _dot(ovt, p_lo)
    jj = lax.broadcasted_iota(I32, (nblk, 1), 0)
    qp1 = t0 + lax.broadcasted_iota(I32, (1, tq), 1)
    cur = qp1 // SEL_BLOCK
    forced = jnp.logical_or(jj == 0, jnp.logical_or(jj == cur, jj == cur - 1))
    score = jnp.where(forced, BIG, jnp.where(jj * SEL_BLOCK <= qp1, imp, NEG))
    ahead = jnp.zeros((nblk, tq), F32)
    for jp in range(nblk):
        sj = score[jp:jp + 1, :]
        first = jnp.logical_or(sj > score, jnp.logical_and(sj == score, jp < jj))
        ahead = ahead + first.astype(F32)
    selb = jnp.where(ahead < float(min(N_SEL, nblk)), 0.0, NEG)
    selb = jnp.concatenate([selb] * HPG, axis=1)
    pad = jnp.zeros((LANES - HEAD_DIM - nblk, nq), F32)
    qx = jnp.concatenate([qrt.astype(F32), selb, pad], axis=0).astype(BF16)

    kc = SLC_KC
    n = t0 // kc + 1
    kd = pl.multiple_of((n - 1) * kc, kc)
    sbufs = (s0_ref, s1_ref)
    ebufs = (e0_ref, e1_ref)
    s0_ref[...] = _dot(ks_ref[0, 0, pl.ds(kd, kc), :], qx) + diagb_ref[(t0 - kd) // tq]
    e1_ref[...] = jnp.zeros((kc, nq), BF16)

    def chunk_start(c):
        return pl.multiple_of(jnp.where(c <= 0, kd, (c - 1) * kc), kc)

    def pv(c, e_ref):
        return _dot(vst_ref[0, 0, :, pl.ds(chunk_start(c), kc)], e_ref[...])

    def step(i, par, carry):
        m, acc, alpha_p = carry
        acc = alpha_p * acc + pv(i - 2, ebufs[par])
        sp = sbufs[1 - par][...]
        m_new = jnp.maximum(m, jnp.max(sp, axis=0, keepdims=True))
        alpha = jnp.exp2(m - m_new)
        ebufs[1 - par][...] = jnp.exp2(sp - m_new).astype(BF16)
        k0 = chunk_start(jnp.minimum(i, n - 1))
        sbufs[par][...] = _dot(ks_ref[0, 0, pl.ds(k0, kc), :], qx)
        return m_new, acc, alpha

    def pair(p, carry):
        return step(2 * p + 2, 0, step(2 * p + 1, 1, carry))

    init = (jnp.full((1, nq), NEG, F32), jnp.zeros((VT_ROWS, nq), F32), jnp.ones((1, nq), F32))
    carry = lax.fori_loop(0, n // 2, pair, init)
    odd = n % 2 == 1
    _, acc, alpha_p = lax.cond(odd, lambda c: step(n, 1, c), lambda c: c, carry)
    acc = alpha_p * acc + jnp.where(odd, pv(n - 1, e0_ref), pv(n - 1, e1_ref))
    o_slc = acc[:HEAD_DIM] * (1.0 / acc[HEAD_DIM:HEAD_DIM + 1])

    gtt = gt_ref[0].T

    def gate(k):
        return jnp.concatenate([gtt[h * 3 + k:h * 3 + k + 1, :] for h in range(HPG)], axis=1)

    o = gate(0) * o_cmp + gate(1) * o_slc + gate(2) * o_win
    o = jnp.concatenate([o[:, h * tq:(h + 1) * tq] for h in range(HPG)], axis=0)
    o_ref[0] = o.T.astype(BF16)


def _attn_masks(tq):
    nq = HPG * tq
    lq = (jnp.arange(nq) % tq)[None, :]
    kr = jnp.arange(SLC_KC)[:, None]
    diag = jnp.stack([jnp.where(kr <= d * tq + lq, 0.0, NEG) for d in range(max(SLC_KC // tq, 1))])
    kr = jnp.arange(WINDOW + tq)[:, None]
    win = []
    for w in range(WINDOW // tq + 1):
        delta = w * tq + lq - kr
        win.append(jnp.where((delta >= 0) & (delta < WINDOW), 0.0, NEG))
    return diag.astype(F32), jnp.stack(win).astype(F32)


def _nsa_attention(q, qr, gates, cmp, cmp_t, ks, vst, kw, vwt, ovt, masks):
    bsz, seq, _ = q.shape
    ng, dh = kw.shape[1], kw.shape[3]
    tq = min(ATT_TQ, seq)
    nq = HPG * tq
    nblk = seq // SEL_BLOCK
    ncp = cmp.shape[3]
    diagb, winb = masks
    qspec = pl.BlockSpec((1, tq, HPG * dh), lambda b, g, i: (b, i, g))
    kern = functools.partial(_attn_kernel, tq=tq, nblk=nblk)
    return pl.pallas_call(
        kern,
        grid=(bsz, ng, seq // tq),
        in_specs=[
            qspec, qspec,
            pl.BlockSpec((1, tq, LANES), lambda b, g, i: (b, i, g)),
            pl.BlockSpec((1, 1, 1, ncp, dh), lambda b, g, i: (0, b, g, 0, 0)),
            pl.BlockSpec((1, 1, 1, dh, ncp), lambda b, g, i: (1, b, g, 0, 0)),
            pl.BlockSpec((1, 1, seq, LANES), lambda b, g, i: (b, g, 0, 0)),
            pl.BlockSpec((1, 1, VT_ROWS, seq), lambda b, g, i: (b, g, 0, 0)),
            pl.BlockSpec((1, 1, seq, dh), lambda b, g, i: (b, g, 0, 0)),
            pl.BlockSpec((1, 1, VT_ROWS, seq), lambda b, g, i: (b, g, 0, 0)),
            pl.BlockSpec((nblk, ncp), lambda b, g, i: (0, 0)),
            pl.BlockSpec(diagb.shape, lambda b, g, i: (0, 0, 0)),
            pl.BlockSpec(winb.shape, lambda b, g, i: (0, 0, 0)),
        ],
        out_specs=qspec,
        out_shape=jax.ShapeDtypeStruct((bsz, seq, ng * HPG * dh), BF16),
        scratch_shapes=[pltpu.VMEM((SLC_KC, nq), F32), pltpu.VMEM((SLC_KC, nq), F32),
                        pltpu.VMEM((SLC_KC, nq), BF16), pltpu.VMEM((SLC_KC, nq), BF16)],
        compiler_params=_cparams(("parallel", "parallel", "arbitrary")),
        name="nsa_attention",
    )(q, qr, gates, cmp, cmp_t, ks, vst, kw, vwt, ovt, diagb, winb)


def _overlap_t(seq):
    ncp = seq // CMP_STRIDE
    nblk = seq // SEL_BLOCK
    c_start = jnp.arange(ncp) * CMP_STRIDE
    j_start = jnp.arange(nblk) * SEL_BLOCK
    ov = (c_start[None, :] < j_start[:, None] + SEL_BLOCK) & (c_start[None, :] + CMP_BLOCK > j_start[:, None])
    return ov.astype(F32)


def _shared_kv(h3, w_kv, cmp_pos, cmp_w1, cmp_w2, tabs):
    raw, ks, vst, kw, vwt = _kv_proj(h3, w_kv, tabs)
    cmp, cmp_t = _compress(raw, cmp_w1, cmp_pos, cmp_w2)
    return cmp, cmp_t, ks, vst, kw, vwt


def _nsa_layer(h2, w_qg, w_o, shared, tabs, ovt, masks, g, b, bsz, seq):
    q, qr, gates = _qg_proj(h2.reshape(bsz, seq, -1), w_qg, tabs)
    o = _nsa_attention(q, qr, gates, *shared, ovt, masks)
    return _mm_postnorm(o.reshape(bsz * seq, -1), w_o, h2, g, b)


def kernel(x, ln_g, ln_b, lru_w_in, lru_conv_w, lru_conv_b, lru_w_a, lru_b_a, lru_w_i, lru_b_i, lru_lambda,
           lru_w_out, nsa_w_kv, nsa_cmp_pos, nsa_cmp_w1, nsa_cmp_w2, nsa_w_qg, nsa_w_o, ffn_w_gu, ffn_w_down,
           moe_w_router, moe_w_gu, moe_w_down):
    bsz, seq, d = x.shape
    tabs = _rope_tables(seq)
    ovt = _overlap_t(seq)
    masks = _attn_masks(min(ATT_TQ, seq))
    h = x
    shared = None
    for l in range(DEPTH):
        if l < N_A_LAYERS:
            h = _lru_layer(h, lru_w_in[l], lru_conv_w[l], lru_conv_b[l], lru_w_a[l], lru_b_a[l].reshape(-1),
                           lru_w_i[l], lru_b_i[l].reshape(-1), lru_lambda[l], lru_w_out[l],
                           ln_g[l, 0], ln_b[l, 0])
            h2 = h.reshape(bsz * seq, d)
        else:
            lb = l - N_A_LAYERS
            h2 = _nsa_layer(h2, nsa_w_qg[lb], nsa_w_o[lb], shared, tabs, ovt, masks, ln_g[l, 0], ln_b[l, 0],
                            bsz, seq)
        if l % 2 == 0:
            h2 = _dense_ffn(h2, ffn_w_gu, ffn_w_down, l // 2, ln_g[l, 1], ln_b[l, 1])
        else:
            h2 = _moe(h2, moe_w_router[l // 2], moe_w_gu, moe_w_down, l // 2, ln_g[l, 1], ln_b[l, 1])
        h = h2.reshape(bsz, seq, d)
        if l == N_A_LAYERS - 1:
            shared = _shared_kv(h, nsa_w_kv, nsa_cmp_pos, nsa_cmp_w1, nsa_cmp_w2, tabs)
    return h
```

```python
import functools

import jax
import jax.numpy as jnp
from jax import lax
from jax.experimental import pallas as pl
from jax.experimental.pallas import tpu as pltpu

F32 = jnp.float32
BF16 = jnp.bfloat16
I32 = jnp.int32

DEPTH = 4
N_A_LAYERS = DEPTH // 2
LRU_BLOCKS = 8
CONV_WIDTH = 4
LRU_C = 8.0
N_HEADS = 16
N_KV = 4
HPG = N_HEADS // N_KV
HEAD_DIM = 64
ROT_DIM = HEAD_DIM // 4
ROPE_THETA = 500000.0
CMP_BLOCK = 32
CMP_STRIDE = 16
SEL_BLOCK = 64
N_SEL = 8
WINDOW = 256
N_EXPERTS = 8
DN_ALPHA = (2.0 * DEPTH) ** 0.25
LN_EPS = 1e-5
NEG = -1e30
BIG = 1e30
LOG2E = 1.4426950408889634

LANES = 128
SUBLANES = 8
VMEM_LIMIT = 52 * 1024 * 1024

LRU_TS = 256
FFN_TM = 1024
DENSE_TM = 512
DENSE_TF = 1408
ROUTE_TM = 512
PROJ_TM = 512
ATT_TQ = 256
SLC_KC = 256
SEG_ALIGN = 16
VT_ROWS = 80


def _cparams(sem):
    return pltpu.CompilerParams(dimension_semantics=sem, vmem_limit_bytes=VMEM_LIMIT)


def _dot(a, b):
    return jnp.dot(a, b, preferred_element_type=F32)


def _split(a):
    hi = a.astype(BF16)
    lo = (a - hi.astype(F32)).astype(BF16)
    return hi, lo


def _dot_f32(a, b):
    a_hi, a_lo = _split(a)
    b_hi, b_lo = _split(b)
    n = b.shape[1]
    wide = _dot(a_hi, jnp.concatenate([b_hi, b_lo], axis=1))
    return wide[:, :n] + (wide[:, n:] + _dot(a_lo, b_hi))


def _sigmoid(x):
    return 0.5 * jnp.tanh(0.5 * x) + 0.5


def _postnorm(res, y, g, b):
    z = DN_ALPHA * res + y
    mu = jnp.mean(z, axis=-1, keepdims=True)
    zc = z - mu
    var = jnp.mean(zc * zc, axis=-1, keepdims=True)
    return zc * lax.rsqrt(var + LN_EPS) * g + b


def _lru_kernel(x_ref, win_ref, cw_ref, cb_ref, wai_ref, ba_ref, bi_ref, lam_ref, wout_ref,
                g_ref, b_ref, o_ref, hcar_ref, xprev_ref, hg_ref, *, ts, width, nblk):
    bw = width // nblk

    @pl.when(pl.program_id(1) == 0)
    def _():
        hcar_ref[...] = jnp.zeros_like(hcar_ref)
        xprev_ref[...] = jnp.zeros_like(xprev_ref)

    x = x_ref[0]
    u = _dot(x.astype(BF16), win_ref[...])
    xc = u[:, width:]
    xcat = jnp.concatenate([xprev_ref[...], xc], axis=0)
    xr = cb_ref[...] + cw_ref[CONV_WIDTH - 1:CONV_WIDTH, :] * xc
    for s in range(1, CONV_WIDTH):
        xr = xr + cw_ref[CONV_WIDTH - 1 - s:CONV_WIDTH - s, :] * xcat[SUBLANES - s:SUBLANES - s + ts]
    xprev_ref[...] = xc[ts - SUBLANES:ts]

    row = lax.broadcasted_iota(I32, (ts // SUBLANES, SUBLANES, bw), 1)
    for n in range(nblk):
        sl = slice(n * bw, (n + 1) * bw)
        xb = xr[:, sl]
        ri = _dot(xb.astype(BF16), wai_ref[n])
        r = _sigmoid(ri[:, :bw] + ba_ref[:, sl])
        i = _sigmoid(ri[:, bw:] + bi_ref[:, sl])
        lam = lam_ref[:, sl]
        sp = jnp.maximum(-lam, 0.0) + jnp.log(1.0 + jnp.exp(-jnp.abs(lam)))
        log_a = (-LRU_C * r) * sp
        a = jnp.exp(log_a)
        th = jnp.tanh(log_a)
        h = jnp.sqrt(-2.0 * th / (1.0 - th)) * (i * xb)
        a = a.reshape(ts // SUBLANES, SUBLANES, bw)
        h = h.reshape(ts // SUBLANES, SUBLANES, bw)
        d = 1
        while d < SUBLANES:
            keep = row >= d
            a_sh = jnp.where(keep, pltpu.roll(a, d, 1), 1.0)
            h_sh = jnp.where(keep, pltpu.roll(h, d, 1), 0.0)
            h = a * h_sh + h
            a = a * a_sh
            d *= 2
        carry = hcar_ref[:, sl]
        groups = []
        for v in range(ts // SUBLANES):
            hv = h[v] + a[v] * carry
            carry = hv[SUBLANES - 1:SUBLANES, :]
            groups.append(hv)
        h = jnp.concatenate(groups, axis=0)
        hcar_ref[:, sl] = carry
        gate = jax.nn.gelu(u[:, sl])
        hg_ref[:, sl] = (h * gate).astype(BF16)

    y = _dot(hg_ref[...], wout_ref[...])
    o_ref[0] = _postnorm(x, y, g_ref[...], b_ref[...])


def _lru_layer(x, w_in, conv_w, conv_b, w_a, b_a, w_i, b_i, lam, w_out, g, b):
    bsz, seq, d = x.shape
    width = w_out.shape[0]
    nblk = w_a.shape[0]
    ts = min(LRU_TS, seq)
    wai = jnp.concatenate([w_a, w_i], axis=-1).astype(BF16)
    row2 = lambda v: v.reshape(1, -1)
    const2 = lambda bi, ti: (0, 0)
    kern = functools.partial(_lru_kernel, ts=ts, width=width, nblk=nblk)
    return pl.pallas_call(
        kern,
        grid=(bsz, seq // ts),
        in_specs=[
            pl.BlockSpec((1, ts, d), lambda bi, ti: (bi, ti, 0)),
            pl.BlockSpec((d, 2 * width), const2),
            pl.BlockSpec((CONV_WIDTH, width), const2),
            pl.BlockSpec((1, width), const2),
            pl.BlockSpec(wai.shape, lambda bi, ti: (0, 0, 0)),
            pl.BlockSpec((1, width), const2),
            pl.BlockSpec((1, width), const2),
            pl.BlockSpec((1, width), const2),
            pl.BlockSpec((width, d), const2),
            pl.BlockSpec((1, d), const2),
            pl.BlockSpec((1, d), const2),
        ],
        out_specs=pl.BlockSpec((1, ts, d), lambda bi, ti: (bi, ti, 0)),
        out_shape=jax.ShapeDtypeStruct((bsz, seq, d), F32),
        scratch_shapes=[
            pltpu.VMEM((1, width), F32),
            pltpu.VMEM((SUBLANES, width), F32),
            pltpu.VMEM((ts, width), BF16),
        ],
        compiler_params=_cparams(("parallel", "arbitrary")),
        name="lru_layer",
    )(x, w_in.astype(BF16), conv_w, row2(conv_b), wai, row2(b_a), row2(b_i), row2(lam),
      w_out.astype(BF16), row2(g), row2(b))


def _ffn_kernel(te_ref, nu_ref, x_ref, wg_ref, wu_ref, wd_ref, g_ref, b_ref, o_ref, xb_ref, acc_ref,
                *, nf, postnorm):
    i = pl.program_id(0)
    f = pl.program_id(1)
    used = i < nu_ref[0]

    @pl.when(used)
    def _():
        @pl.when(f == 0)
        def _():
            xb_ref[...] = x_ref[...].astype(BF16)
            acc_ref[...] = jnp.zeros_like(acc_ref)

        xb = xb_ref[...]
        gp = _dot(xb, wg_ref[0, 0].astype(BF16))
        up = _dot(xb, wu_ref[0, 0].astype(BF16))
        act = (gp * jax.nn.sigmoid(gp) * up).astype(BF16)
        acc_ref[...] += _dot(act, wd_ref[0, 0].astype(BF16))

        @pl.when(f == nf - 1)
        def _():
            if postnorm:
                o_ref[...] = _postnorm(x_ref[...], acc_ref[...], g_ref[...], b_ref[...])
            else:
                o_ref[...] = acc_ref[...].astype(o_ref.dtype)

    @pl.when(jnp.logical_and(jnp.logical_not(used), f == nf - 1))
    def _():
        o_ref[...] = jnp.zeros_like(o_ref)


def _ffn(x, w_gu, w_down, layer, tile_expert, n_used, g, b, *, postnorm, tf, tm):
    rows, d = x.shape
    ff = w_down.shape[2]
    nf = ff // tf
    nt = rows // tm

    def fidx(i, f, nu):
        return jnp.where(i < nu[0], f, nf - 1)

    kern = functools.partial(_ffn_kernel, nf=nf, postnorm=postnorm)
    grid_spec = pltpu.PrefetchScalarGridSpec(
        num_scalar_prefetch=2,
        grid=(nt, nf),
        in_specs=[
            pl.BlockSpec((tm, d), lambda i, f, te, nu: (i, 0)),
            pl.BlockSpec((1, 1, d, tf), lambda i, f, te, nu: (layer, te[i], 0, fidx(i, f, nu))),
            pl.BlockSpec((1, 1, d, tf), lambda i, f, te, nu: (layer, te[i], 0, fidx(i, f, nu) + nf)),
            pl.BlockSpec((1, 1, tf, d), lambda i, f, te, nu: (layer, te[i], fidx(i, f, nu), 0)),
            pl.BlockSpec((1, d), lambda i, f, te, nu: (0, 0)),
            pl.BlockSpec((1, d), lambda i, f, te, nu: (0, 0)),
        ],
        out_specs=pl.BlockSpec((tm, d), lambda i, f, te, nu: (i, 0)),
        scratch_shapes=[pltpu.VMEM((tm, d), BF16), pltpu.VMEM((tm, d), F32)],
    )
    return pl.pallas_call(
        kern,
        grid_spec=grid_spec,
        out_shape=jax.ShapeDtypeStruct((rows, d), F32 if postnorm else BF16),
        compiler_params=_cparams(("parallel", "arbitrary")),
        name="ffn_postnorm" if postnorm else "ffn_experts",
    )(tile_expert, n_used, x, w_gu, w_gu, w_down, g.reshape(1, -1), b.reshape(1, -1))


def _dense_ffn(h2, w_gu, w_down, layer, g, b):
    rows = h2.shape[0]
    tm = min(DENSE_TM, rows)
    nt = rows // tm
    ff = w_down.shape[1]
    tf = DENSE_TF if ff % DENSE_TF == 0 else LANES
    return _ffn(h2, w_gu[:, None].astype(BF16), w_down[:, None].astype(BF16), layer,
                jnp.zeros((nt,), I32), jnp.full((1,), nt, I32), g, b, postnorm=True, tf=tf, tm=tm)


def _router_kernel(h_ref, wr_ref, ut_ref, xl_ref, mi_ref, mw_ref, seg_ref, *, tm, n_exp, loc):
    x = h_ref[...]
    logits = _dot_f32(x, wr_ref[...])
    lane = lax.broadcasted_iota(I32, (tm, LANES), 1)
    lg = jnp.where(lane < n_exp, logits, -jnp.inf)
    m0 = jnp.max(lg, axis=1, keepdims=True)
    i0 = jnp.min(jnp.where(lg == m0, lane, LANES), axis=1, keepdims=True)
    lg1 = jnp.where(lane == i0, -jnp.inf, lg)
    m1 = jnp.max(lg1, axis=1, keepdims=True)
    i1 = jnp.min(jnp.where(lg1 == m1, lane, LANES), axis=1, keepdims=True)
    e1 = jnp.exp(m1 - m0)
    w0 = 1.0 / (1.0 + e1)
    w1 = e1 / (1.0 + e1)

    oh0 = lane == i0
    oh1 = lane == i1
    oh = jnp.logical_or(oh0, oh1).astype(F32)
    tri = (lax.broadcasted_iota(I32, (tm, tm), 0) > lax.broadcasted_iota(I32, (tm, tm), 1)).astype(BF16)
    before = _dot(tri, oh.astype(BF16))
    cnt = jnp.sum(oh, axis=0, keepdims=True)
    padded = jnp.floor((cnt + (SEG_ALIGN - 1)) * (1.0 / SEG_ALIGN)) * SEG_ALIGN
    seg = _dot(jnp.broadcast_to(padded, (SUBLANES, LANES)).astype(BF16), ut_ref[...].astype(BF16))[0:1, :]
    lp0 = jnp.sum(jnp.where(oh0, seg + before, 0.0), axis=1, keepdims=True)
    lp1 = jnp.sum(jnp.where(oh1, seg + before, 0.0), axis=1, keepdims=True)
    lpt = jnp.where(lane == 0, lp0, jnp.where(lane == 1, lp1, -1.0)).T
    prow = lax.broadcasted_iota(I32, (loc, tm), 0).astype(F32)
    place = jnp.logical_or(prow == lpt[0:1, :], prow == lpt[1:2, :]).astype(BF16)
    xl_ref[0] = _dot(place, x.astype(BF16)).astype(BF16)

    mi_ref[...] = jnp.where(lane == 0, lp0, lp1).astype(I32)
    mw_ref[...] = jnp.where(lane == 0, w0, w1)
    row8 = lax.broadcasted_iota(I32, (SUBLANES, LANES), 0)
    seg_ref[0] = jnp.where(row8 == 0, cnt, seg).astype(I32)


def _router(h2, w_router):
    rows, d = h2.shape
    n_exp = w_router.shape[1]
    tm = min(ROUTE_TM, rows)
    nt = rows // tm
    loc = 2 * tm + LANES
    assert n_exp * (SEG_ALIGN - 1) <= LANES
    wr = jnp.pad(w_router, ((0, 0), (0, LANES - n_exp)))
    upper = (jnp.arange(LANES)[:, None] < jnp.arange(LANES)[None, :]).astype(F32)
    kern = functools.partial(_router_kernel, tm=tm, n_exp=n_exp, loc=loc)
    return pl.pallas_call(
        kern,
        grid=(nt,),
        in_specs=[pl.BlockSpec((tm, d), lambda i: (i, 0)), pl.BlockSpec((d, LANES), lambda i: (0, 0)),
                  pl.BlockSpec((LANES, LANES), lambda i: (0, 0))],
        out_specs=[
            pl.BlockSpec((1, loc, d), lambda i: (i, 0, 0)),
            pl.BlockSpec((tm, LANES), lambda i: (i, 0)),
            pl.BlockSpec((tm, LANES), lambda i: (i, 0)),
            pl.BlockSpec((1, SUBLANES, LANES), lambda i: (i, 0, 0)),
        ],
        out_shape=[
            jax.ShapeDtypeStruct((nt, loc, d), BF16),
            jax.ShapeDtypeStruct((rows, LANES), I32),
            jax.ShapeDtypeStruct((rows, LANES), F32),
            jax.ShapeDtypeStruct((nt, SUBLANES, LANES), I32),
        ],
        compiler_params=_cparams(("parallel",)),
        name="moe_router",
    )(h2, wr, upper)


def _segment_copies(loc_ref, srt_ref, units_ref, local_buf, sorted_hbm, sem, tile, n_exp, nbits, to_sorted, do):
    base = tile * n_exp
    for e in range(n_exp):
        lo = loc_ref[base + e]
        so = srt_ref[base + e]
        units = units_ref[base + e]
        off = 0
        for k in reversed(range(nbits)):
            size = SEG_ALIGN << k
            bit = (units >> k) & 1

            @pl.when(bit == 1)
            def _(off=off, size=size, lo=lo, so=so):
                local = local_buf.at[pl.ds(pl.multiple_of(lo + off, SEG_ALIGN), size)]
                srt = sorted_hbm.at[pl.ds(pl.multiple_of(so + off, SEG_ALIGN), size)]
                do(pltpu.make_async_copy(local, srt, sem) if to_sorted else pltpu.make_async_copy(srt, local, sem))

            off = off + bit * size


def _spread_kernel(loc_ref, srt_ref, units_ref, xl_ref, zin_ref, o_ref, sem, *, n_exp, nbits):
    del zin_ref
    for do in (lambda cp: cp.start(), lambda cp: cp.wait()):
        _segment_copies(loc_ref, srt_ref, units_ref, xl_ref.at[0], o_ref, sem, pl.program_id(0), n_exp, nbits,
                        True, do)


def _spread_segments(x_loc, seg_start, sorted_start, units, n_out, nbits):
    nt, loc, d = x_loc.shape
    n_exp = units.shape[1]
    kern = functools.partial(_spread_kernel, n_exp=n_exp, nbits=nbits)
    grid_spec = pltpu.PrefetchScalarGridSpec(
        num_scalar_prefetch=3,
        grid=(nt,),
        in_specs=[pl.BlockSpec((1, loc, d), lambda i, a, b, c: (i, 0, 0)), pl.BlockSpec(memory_space=pl.ANY)],
        out_specs=pl.BlockSpec(memory_space=pl.ANY),
        scratch_shapes=[pltpu.SemaphoreType.DMA(())],
    )
    return pl.pallas_call(
        kern,
        grid_spec=grid_spec,
        out_shape=jax.ShapeDtypeStruct((n_out, d), x_loc.dtype),
        input_output_aliases={4: 0},
        compiler_params=_cparams(("arbitrary",)),
        name="moe_spread",
    )(seg_start.reshape(-1), sorted_start.reshape(-1), units.reshape(-1), x_loc, jnp.zeros((n_out, d), x_loc.dtype))


def _combine_kernel(loc_ref, srt_ref, units_ref, mi_ref, mw_ref, h_ref, ys_ref, g_ref, b_ref, o_ref, yl_ref, sem,
                    *, tm, n_exp, nbits):
    loc = yl_ref.shape[1]
    i = pl.program_id(0)
    slot = i % 2

    def copies(tile, buf, do):
        _segment_copies(loc_ref, srt_ref, units_ref, yl_ref.at[buf], ys_ref, sem.at[buf], tile, n_exp, nbits,
                        False, do)

    @pl.when(i == 0)
    def _():
        yl_ref[...] = jnp.zeros_like(yl_ref)
        copies(0, 0, lambda cp: cp.start())

    @pl.when(i + 1 < pl.num_programs(0))
    def _():
        copies(i + 1, 1 - slot, lambda cp: cp.start())

    copies(i, slot, lambda cp: cp.wait())
    mi = mi_ref[...]
    w = mw_ref[...]
    col = lax.broadcasted_iota(I32, (tm, loc), 1)
    wmat = jnp.where(col == mi[:, 0:1], w[:, 0:1], 0.0) + jnp.where(col == mi[:, 1:2], w[:, 1:2], 0.0)
    w_hi, w_lo = _split(wmat)
    yl = yl_ref[slot]
    y = _dot(w_hi, yl) + _dot(w_lo, yl)
    o_ref[...] = _postnorm(h_ref[...], y, g_ref[...], b_ref[...])


def _combine(h2, ys, seg_start, sorted_start, units, loc, mi, mw, g, b, nbits):
    rows, d = h2.shape
    nt, n_exp = units.shape
    tm = rows // nt
    kern = functools.partial(_combine_kernel, tm=tm, n_exp=n_exp, nbits=nbits)
    grid_spec = pltpu.PrefetchScalarGridSpec(
        num_scalar_prefetch=3,
        grid=(nt,),
        in_specs=[
            pl.BlockSpec((tm, LANES), lambda i, a, b, c: (i, 0)),
            pl.BlockSpec((tm, LANES), lambda i, a, b, c: (i, 0)),
            pl.BlockSpec((tm, d), lambda i, a, b, c: (i, 0)),
            pl.BlockSpec(memory_space=pl.ANY),
            pl.BlockSpec((1, d), lambda i, a, b, c: (0, 0)),
            pl.BlockSpec((1, d), lambda i, a, b, c: (0, 0)),
        ],
        out_specs=pl.BlockSpec((tm, d), lambda i, a, b, c: (i, 0)),
        scratch_shapes=[pltpu.VMEM((2, loc, d), ys.dtype), pltpu.SemaphoreType.DMA((2,))],
    )
    return pl.pallas_call(
        kern,
        grid_spec=grid_spec,
        out_shape=jax.ShapeDtypeStruct((rows, d), F32),
        compiler_params=_cparams(("arbitrary",)),
        name="moe_combine",
    )(seg_start.reshape(-1), sorted_start.reshape(-1), units.reshape(-1), mi, mw, h2, ys,
      g.reshape(1, -1), b.reshape(1, -1))


def _moe(h2, w_router, w_gu, w_down, layer, g, b):
    rows, d = h2.shape
    n_exp = w_router.shape[1]
    tm = min(FFN_TM, rows)
    x_loc, mi, mw, seg = _router(h2, w_router)
    nt_r, loc, _ = x_loc.shape
    counts = seg[:, 0, :n_exp]
    seg_start = seg[:, 1, :n_exp]
    units = (counts + SEG_ALIGN - 1) // SEG_ALIGN
    seg_len = units * SEG_ALIGN
    group_rows = jnp.sum(seg_len, axis=0)
    tiles_per = (group_rows + tm - 1) // tm
    tile_end = jnp.cumsum(tiles_per)
    group_start = (tile_end - tiles_per) * tm
    sorted_start = group_start[None, :] + jnp.cumsum(seg_len, axis=0) - seg_len
    n_tiles = -(-(2 * rows + nt_r * n_exp * (SEG_ALIGN - 1)) // tm) + n_exp
    n_used = tile_end[-1]
    tid = jnp.minimum(jnp.arange(n_tiles, dtype=I32), n_used - 1)
    tile_expert = jnp.sum((tid[:, None] >= tile_end[None, :]).astype(I32), axis=1)
    tile_expert = jnp.minimum(tile_expert, n_exp - 1)
    nbits = (min(ROUTE_TM, rows) // SEG_ALIGN).bit_length()

    xs = _spread_segments(x_loc, seg_start, sorted_start, units, n_tiles * tm, nbits)
    ys = _ffn(xs, w_gu, w_down, layer, tile_expert, n_used.reshape(1), g, b, postnorm=False, tf=512, tm=tm)
    return _combine(h2, ys, seg_start, sorted_start, units, loc, mi, mw, g, b, nbits)


def _rope_cols(x, c, sa, sb):
    half = ROT_DIM // 2
    return x * c + pltpu.roll(x, LANES - half, 1) * sa + pltpu.roll(x, half, 1) * sb


def _rope_tables(seq):
    half = ROT_DIM // 2
    inv = ROPE_THETA ** (-jnp.arange(half, dtype=F32) / half)
    ang = jnp.arange(seq, dtype=F32)[:, None] * inv[None]
    cos, sin = jnp.cos(ang), jnp.sin(ang)
    ones = jnp.ones((seq, HEAD_DIM - ROT_DIM), F32)
    zeros = jnp.zeros((seq, HEAD_DIM - ROT_DIM), F32)
    zh = jnp.zeros((seq, half), F32)
    c = jnp.concatenate([cos, cos, ones], axis=1)
    sa = jnp.concatenate([-sin, zh, zeros], axis=1)
    sb = jnp.concatenate([zh, sin, zeros], axis=1)
    rep = LANES // HEAD_DIM
    return tuple(jnp.tile(t, (1, rep)) for t in (c, sa, sb))


def _kv_kernel(h_ref, w_ref, c_ref, sa_ref, sb_ref, raw_ref, ks_ref, vst_ref, kw_ref, vwt_ref, *, gw):
    res = _dot(h_ref[0].astype(BF16), w_ref[...])
    c, sa, sb = c_ref[...], sa_ref[...], sb_ref[...]
    ng = gw // HEAD_DIM

    def part(p):
        return res[:, p * gw:(p + 1) * gw]

    def roped(p):
        x = part(p)
        return jnp.concatenate(
            [_rope_cols(x[:, j * LANES:(j + 1) * LANES], c, sa, sb) for j in range(gw // LANES)], axis=1)

    for t in range(2):
        x = part(t)
        for g in range(ng):
            raw_ref[t, 0, g] = x[:, g * HEAD_DIM:(g + 1) * HEAD_DIM]
    tm = res.shape[0]
    pos = pl.program_id(1) * tm + lax.broadcasted_iota(I32, (tm, LANES - HEAD_DIM), 0)
    blk_onehot = (pos // SEL_BLOCK == lax.broadcasted_iota(I32, (tm, LANES - HEAD_DIM), 1)).astype(F32)
    ones_rows = (lax.broadcasted_iota(I32, (VT_ROWS - HEAD_DIM, tm), 0) == 0).astype(F32)
    for p, k_ref, vt_ref in ((2, ks_ref, vst_ref), (4, kw_ref, vwt_ref)):
        k = roped(p)
        vt = part(p + 1).T
        for g in range(ng):
            kg = k[:, g * HEAD_DIM:(g + 1) * HEAD_DIM]
            if p == 2:
                kg = jnp.concatenate([kg, blk_onehot], axis=1)
            k_ref[0, g] = kg.astype(BF16)
            vt_ref[0, g] = jnp.concatenate([vt[g * HEAD_DIM:(g + 1) * HEAD_DIM, :], ones_rows], axis=0).astype(BF16)


def _kv_proj(h3, w_kv, tabs):
    bsz, seq, d = h3.shape
    n = w_kv.shape[1]
    gw = n // 6
    ng = gw // HEAD_DIM
    tm = min(PROJ_TM, seq)
    assert seq // SEL_BLOCK <= LANES - HEAD_DIM
    tab_spec = pl.BlockSpec((tm, LANES), lambda b, i: (i, 0))

    def k_spec(w):
        return pl.BlockSpec((1, ng, tm, w), lambda b, i: (b, 0, i, 0))

    def k_shape(w):
        return jax.ShapeDtypeStruct((bsz, ng, seq, w), BF16)

    vt_spec = pl.BlockSpec((1, ng, VT_ROWS, tm), lambda b, i: (b, 0, 0, i))
    vt_shape = jax.ShapeDtypeStruct((bsz, ng, VT_ROWS, seq), BF16)
    kern = functools.partial(_kv_kernel, gw=gw)
    return pl.pallas_call(
        kern,
        grid=(bsz, seq // tm),
        in_specs=[pl.BlockSpec((1, tm, d), lambda b, i: (b, i, 0)), pl.BlockSpec((d, n), lambda b, i: (0, 0)),
                  tab_spec, tab_spec, tab_spec],
        out_specs=[pl.BlockSpec((2, 1, ng, tm, HEAD_DIM), lambda b, i: (0, b, 0, i, 0)),
                   k_spec(LANES), vt_spec, k_spec(HEAD_DIM), vt_spec],
        out_shape=[jax.ShapeDtypeStruct((2, bsz, ng, seq, HEAD_DIM), F32),
                   k_shape(LANES), vt_shape, k_shape(HEAD_DIM), vt_shape],
        compiler_params=_cparams(("parallel", "parallel")),
        name="kv_proj",
    )(h3, w_kv.astype(BF16), *tabs)


def _qg_kernel(h_ref, w_ref, c_ref, sa_ref, sb_ref, q_ref, qr_ref, gt_ref, *, nq):
    res = _dot(h_ref[0].astype(BF16), w_ref[...])
    c, sa, sb = c_ref[...], sa_ref[...], sb_ref[...]
    scale = HEAD_DIM ** -0.5 * LOG2E
    for j in range(nq // LANES):
        x = res[:, j * LANES:(j + 1) * LANES] * scale
        q_ref[0, :, j * LANES:(j + 1) * LANES] = x.astype(BF16)
        qr_ref[0, :, j * LANES:(j + 1) * LANES] = _rope_cols(x, c, sa, sb).astype(BF16)
    gt_ref[0] = jax.nn.sigmoid(res[:, nq:])


def _qg_proj(h3, w_qg, tabs):
    bsz, seq, d = h3.shape
    nq = N_HEADS * HEAD_DIM
    ngate = HPG * 3
    tm = min(PROJ_TM, seq)
    wg = w_qg[:, nq:].reshape(d, N_KV, ngate)
    wg = jnp.pad(wg, ((0, 0), (0, 0), (0, LANES - ngate))).reshape(d, N_KV * LANES)
    w = jnp.concatenate([w_qg[:, :nq], wg], axis=1).astype(BF16)
    n = w.shape[1]
    tab_spec = pl.BlockSpec((tm, LANES), lambda b, i: (i, 0))
    kern = functools.partial(_qg_kernel, nq=nq)
    return pl.pallas_call(
        kern,
        grid=(bsz, seq // tm),
        in_specs=[pl.BlockSpec((1, tm, d), lambda b, i: (b, i, 0)), pl.BlockSpec((d, n), lambda b, i: (0, 0)),
                  tab_spec, tab_spec, tab_spec],
        out_specs=[pl.BlockSpec((1, tm, nq), lambda b, i: (b, i, 0)), pl.BlockSpec((1, tm, nq), lambda b, i: (b, i, 0)),
                   pl.BlockSpec((1, tm, N_KV * LANES), lambda b, i: (b, i, 0))],
        out_shape=[jax.ShapeDtypeStruct((bsz, seq, nq), BF16), jax.ShapeDtypeStruct((bsz, seq, nq), BF16),
                   jax.ShapeDtypeStruct((bsz, seq, N_KV * LANES), F32)],
        compiler_params=_cparams(("parallel", "parallel")),
        name="qg_proj",
    )(h3, w, *tabs)


def _mm_ln_kernel(a_ref, w_ref, res_ref, g_ref, b_ref, o_ref):
    half = a_ref.shape[0] // 2
    for r in (slice(0, half), slice(half, 2 * half)):
        y = _dot(a_ref[r, :], w_ref[...])
        o_ref[r, :] = _postnorm(res_ref[r, :], y, g_ref[...], b_ref[...])


def _mm_postnorm(a, w, res, g, b):
    rows, k = a.shape
    d = w.shape[1]
    tm = min(PROJ_TM, rows)
    return pl.pallas_call(
        _mm_ln_kernel,
        grid=(rows // tm,),
        in_specs=[pl.BlockSpec((tm, k), lambda i: (i, 0)), pl.BlockSpec((k, d), lambda i: (0, 0)),
                  pl.BlockSpec((tm, d), lambda i: (i, 0)),
                  pl.BlockSpec((1, d), lambda i: (0, 0)), pl.BlockSpec((1, d), lambda i: (0, 0))],
        out_specs=pl.BlockSpec((tm, d), lambda i: (i, 0)),
        out_shape=jax.ShapeDtypeStruct((rows, d), F32),
        compiler_params=_cparams(("parallel",)),
        name="out_proj_postnorm",
    )(a, w.astype(BF16), res, g.reshape(1, -1), b.reshape(1, -1))


def _compress_kernel(raw_ref, w1_ref, pe_ref, w2_ref, o_ref, ot_ref, *, rows_per):
    half = w1_ref.shape[1] // 2
    pieces = [raw_ref[0, 0, 0, pl.ds(l, rows_per, stride=CMP_STRIDE), :] for l in range(CMP_STRIDE)]
    x = jnp.concatenate(pieces, axis=1).astype(BF16)
    first = _dot(x, w1_ref[0, :half, :])
    second = _dot(x, w1_ref[0, half:, :])
    bias = _dot(pe_ref[0].astype(BF16), w1_ref[0])[0:1, :]
    hid = first + pltpu.roll(second, rows_per - 1, 0) + bias
    out = _dot(jax.nn.gelu(hid).astype(BF16), w2_ref[0])
    o_ref[0, 0, 0] = out.astype(BF16)
    wide = jnp.concatenate([out, jnp.zeros_like(out)], axis=1)
    ot_ref[0, 0, 0] = wide.T[:out.shape[1], :].astype(BF16)


def _compress(raw, w1, pe, w2):
    _, bsz, ng, seq, dh = raw.shape
    rows_per = seq // CMP_STRIDE
    k2 = w1.shape[1]
    hid = w1.shape[2]
    pe8 = jnp.broadcast_to(pe.reshape(2, 1, k2), (2, SUBLANES, k2))
    kern = functools.partial(_compress_kernel, rows_per=rows_per)
    return pl.pallas_call(
        kern,
        grid=(2, bsz, ng),
        in_specs=[
            pl.BlockSpec((1, 1, 1, seq, dh), lambda t, b, g: (t, b, g, 0, 0)),
            pl.BlockSpec((1, k2, hid), lambda t, b, g: (t, 0, 0)),
            pl.BlockSpec((1, SUBLANES, k2), lambda t, b, g: (t, 0, 0)),
            pl.BlockSpec((1, hid, dh), lambda t, b, g: (t, 0, 0)),
        ],
        out_specs=[pl.BlockSpec((1, 1, 1, rows_per, dh), lambda t, b, g: (t, b, g, 0, 0)),
                   pl.BlockSpec((1, 1, 1, dh, rows_per), lambda t, b, g: (t, b, g, 0, 0))],
        out_shape=[jax.ShapeDtypeStruct((2, bsz, ng, rows_per, dh), BF16),
                   jax.ShapeDtypeStruct((2, bsz, ng, dh, rows_per), BF16)],
        compiler_params=_cparams(("parallel", "parallel", "parallel")),
        name="kv_compress",
    )(raw, w1.astype(BF16), pe8, w2.astype(BF16))


def _softmax2_cols(s):
    m = jnp.max(s, axis=0, keepdims=True)
    e = jnp.exp2(s - m)
    return e * (1.0 / jnp.sum(e, axis=0, keepdims=True))


def _heads_t(x):
    xt = x.astype(F32).T
    return jnp.concatenate([xt[h * HEAD_DIM:(h + 1) * HEAD_DIM, :] for h in range(HPG)], axis=1)


def _attn_kernel(q_ref, qr_ref, gt_ref, kc_ref, vct_ref, ks_ref, vst_ref, kw_ref, vwt_ref, ovt_ref,
                 diagb_ref, winb_ref, o_ref, s0_ref, s1_ref, e0_ref, e1_ref, *, tq, nblk):
    nq = HPG * tq
    it = pl.program_id(2)
    t0 = it * tq
    lane_q = lax.broadcasted_iota(I32, (1, nq), 1)
    qpos = t0 + lane_q % tq
    qt = _heads_t(q_ref[0]).astype(BF16)
    qrt = _heads_t(qr_ref[0]).astype(BF16)

    wk = WINDOW + tq
    nwin = WINDOW // tq
    k0 = pl.multiple_of(jnp.maximum(t0 - WINDOW, 0), tq)
    s = _dot(kw_ref[0, 0, pl.ds(k0, wk), :], qrt) + winb_ref[jnp.minimum(it, nwin)]
    ew = jnp.exp2(s - jnp.max(s, axis=0, keepdims=True))
    ow = _dot(vwt_ref[0, 0, :, pl.ds(k0, wk)], ew.astype(BF16))
    o_win = ow[:HEAD_DIM] * (1.0 / ow[HEAD_DIM:HEAD_DIM + 1])

    ncp = kc_ref.shape[3]
    sc = _dot(kc_ref[0, 0, 0], qt)
    blk_end = lax.broadcasted_iota(I32, (ncp, 1), 0) * CMP_STRIDE + (CMP_BLOCK - 1)
    p = _softmax2_cols(jnp.where(blk_end <= qpos, sc, NEG))
    p = p * (qpos >= CMP_BLOCK - 1).astype(F32)
    o_cmp = _dot(vct_ref[0, 0, 0], p.astype(BF16))

    psum = p[:, 0:tq]
    for hh in range(1, HPG):
        psum = psum + p[:, hh * tq:(hh + 1) * tq]
    p_hi, p_lo = _split(psum)
    ovt = ovt_ref[...].astype(BF16)
    imp = _dot(ovt, p_hi) + _dot(ovt, p_lo)
    jj = lax.broadcasted_iota(I32, (nblk, 1), 0)
    qp1 = t0 + lax.broadcasted_iota(I32, (1, tq), 1)
    cur = qp1 // SEL_BLOCK
    forced = jnp.logical_or(jj == 0, jnp.logical_or(jj == cur, jj == cur - 1))
    score = jnp.where(forced, BIG, jnp.where(jj * SEL_BLOCK <= qp1, imp, NEG))
    ahead = jnp.zeros((nblk, tq), F32)
    for jp in range(nblk):
        sj = score[jp:jp + 1, :]
        first = jnp.logical_or(sj > score, jnp.logical_and(sj == score, jp < jj))
        ahead = ahead + first.astype(F32)
    selb = jnp.where(ahead < float(min(N_SEL, nblk)), 0.0, NEG)
    selb = jnp.concatenate([selb] * HPG, axis=1)
    pad = jnp.zeros((LANES - HEAD_DIM - nblk, nq), F32)
    qx = jnp.concatenate([qrt.astype(F32), selb, pad], axis=0).astype(BF16)

    kc = SLC_KC
    n = t0 // kc + 1
    kd = pl.multiple_of((n - 1) * kc, kc)
    sbufs = (s0_ref, s1_ref)
    ebufs = (e0_ref, e1_ref)
    s0_ref[...] = _dot(ks_ref[0, 0, pl.ds(kd, kc), :], qx) + diagb_ref[(t0 - kd) // tq]
    e1_ref[...] = jnp.zeros((kc, nq), BF16)

    def chunk_start(c):
        return pl.multiple_of(jnp.where(c <= 0, kd, (c - 1) * kc), kc)

    def pv(c, e_ref):
        return _dot(vst_ref[0, 0, :, pl.ds(chunk_start(c), kc)], e_ref[...])

    def step(i, par, carry):
        m, acc, alpha_p = carry
        acc = alpha_p * acc + pv(i - 2, ebufs[par])
        sp = sbufs[1 - par][...]
        m_new = jnp.maximum(m, jnp.max(sp, axis=0, keepdims=True))
        alpha = jnp.exp2(m - m_new)
        ebufs[1 - par][...] = jnp.exp2(sp - m_new).astype(BF16)
        k0 = chunk_start(jnp.minimum(i, n - 1))
        sbufs[par][...] = _dot(ks_ref[0, 0, pl.ds(k0, kc), :], qx)
        return m_new, acc, alpha

    def pair(p, carry):
        return step(2 * p + 2, 0, step(2 * p + 1, 1, carry))

    init = (jnp.full((1, nq), NEG, F32), jnp.zeros((VT_ROWS, nq), F32), jnp.ones((1, nq), F32))
    carry = lax.fori_loop(0, n // 2, pair, init)
    odd = n % 2 == 1
    _, acc, alpha_p = lax.cond(odd, lambda c: step(n, 1, c), lambda c: c, carry)
    acc = alpha_p * acc + jnp.where(odd, pv(n - 1, e0_ref), pv(n - 1, e1_ref))
    o_slc = acc[:HEAD_DIM] * (1.0 / acc[HEAD_DIM:HEAD_DIM + 1])

    gtt = gt_ref[0].T

    def gate(k):
        return jnp.concatenate([gtt[h * 3 + k:h * 3 + k + 1, :] for h in range(HPG)], axis=1)

    o = gate(0) * o_cmp + gate(1) * o_slc + gate(2) * o_win
    o = jnp.concatenate([o[:, h * tq:(h + 1) * tq] for h in range(HPG)], axis=0)
    o_ref[0] = o.T.astype(BF16)


def _attn_masks(tq):
    nq = HPG * tq
    lq = (jnp.arange(nq) % tq)[None, :]
    kr = jnp.arange(SLC_KC)[:, None]
    diag = jnp.stack([jnp.where(kr <= d * tq + lq, 0.0, NEG) for d in range(max(SLC_KC // tq, 1))])
    kr = jnp.arange(WINDOW + tq)[:, None]
    win = []
    for w in range(WINDOW // tq + 1):
        delta = w * tq + lq - kr
        win.append(jnp.where((delta >= 0) & (delta < WINDOW), 0.0, NEG))
    return diag.astype(F32), jnp.stack(win).astype(F32)


def _nsa_attention(q, qr, gates, cmp, cmp_t, ks, vst, kw, vwt, ovt, masks):
    bsz, seq, _ = q.shape
    ng, dh = kw.shape[1], kw.shape[3]
    tq = min(ATT_TQ, seq)
    nq = HPG * tq
    nblk = seq // SEL_BLOCK
    ncp = cmp.shape[3]
    diagb, winb = masks
    qspec = pl.BlockSpec((1, tq, HPG * dh), lambda b, g, i: (b, i, g))
    kern = functools.partial(_attn_kernel, tq=tq, nblk=nblk)
    return pl.pallas_call(
        kern,
        grid=(bsz, ng, seq // tq),
        in_specs=[
            qspec, qspec,
            pl.BlockSpec((1, tq, LANES), lambda b, g, i: (b, i, g)),
            pl.BlockSpec((1, 1, 1, ncp, dh), lambda b, g, i: (0, b, g, 0, 0)),
            pl.BlockSpec((1, 1, 1, dh, ncp), lambda b, g, i: (1, b, g, 0, 0)),
            pl.BlockSpec((1, 1, seq, LANES), lambda b, g, i: (b, g, 0, 0)),
            pl.BlockSpec((1, 1, VT_ROWS, seq), lambda b, g, i: (b, g, 0, 0)),
            pl.BlockSpec((1, 1, seq, dh), lambda b, g, i: (b, g, 0, 0)),
            pl.BlockSpec((1, 1, VT_ROWS, seq), lambda b, g, i: (b, g, 0, 0)),
            pl.BlockSpec((nblk, ncp), lambda b, g, i: (0, 0)),
            pl.BlockSpec(diagb.shape, lambda b, g, i: (0, 0, 0)),
            pl.BlockSpec(winb.shape, lambda b, g, i: (0, 0, 0)),
        ],
        out_specs=qspec,
        out_shape=jax.ShapeDtypeStruct((bsz, seq, ng * HPG * dh), BF16),
        scratch_shapes=[pltpu.VMEM((SLC_KC, nq), F32), pltpu.VMEM((SLC_KC, nq), F32),
                        pltpu.VMEM((SLC_KC, nq), BF16), pltpu.VMEM((SLC_KC, nq), BF16)],
        compiler_params=_cparams(("parallel", "parallel", "arbitrary")),
        name="nsa_attention",
    )(q, qr, gates, cmp, cmp_t, ks, vst, kw, vwt, ovt, diagb, winb)


def _overlap_t(seq):
    ncp = seq // CMP_STRIDE
    nblk = seq // SEL_BLOCK
    c_start = jnp.arange(ncp) * CMP_STRIDE
    j_start = jnp.arange(nblk) * SEL_BLOCK
    ov = (c_start[None, :] < j_start[:, None] + SEL_BLOCK) & (c_start[None, :] + CMP_BLOCK > j_start[:, None])
    return ov.astype(F32)


def _shared_kv(h3, w_kv, cmp_pos, cmp_w1, cmp_w2, tabs):
    raw, ks, vst, kw, vwt = _kv_proj(h3, w_kv, tabs)
    cmp, cmp_t = _compress(raw, cmp_w1, cmp_pos, cmp_w2)
    return cmp, cmp_t, ks, vst, kw, vwt


def _nsa_layer(h2, w_qg, w_o, shared, tabs, ovt, masks, g, b, bsz, seq):
    q, qr, gates = _qg_proj(h2.reshape(bsz, seq, -1), w_qg, tabs)
    o = _nsa_attention(q, qr, gates, *shared, ovt, masks)
    return _mm_postnorm(o.reshape(bsz * seq, -1), w_o, h2, g, b)


def kernel(x, ln_g, ln_b, lru_w_in, lru_conv_w, lru_conv_b, lru_w_a, lru_b_a, lru_w_i, lru_b_i, lru_lambda,
           lru_w_out, nsa_w_kv, nsa_cmp_pos, nsa_cmp_w1, nsa_cmp_w2, nsa_w_qg, nsa_w_o, ffn_w_gu, ffn_w_down,
           moe_w_router, moe_w_gu, moe_w_down):
    bsz, seq, d = x.shape
    tabs = _rope_tables(seq)
    ovt = _overlap_t(seq)
    masks = _attn_masks(min(ATT_TQ, seq))
    h = x
    shared = None
    for l in range(DEPTH):
        if l < N_A_LAYERS:
            h = _lru_layer(h, lru_w_in[l], lru_conv_w[l], lru_conv_b[l], lru_w_a[l], lru_b_a[l].reshape(-1),
                           lru_w_i[l], lru_b_i[l].reshape(-1), lru_lambda[l], lru_w_out[l],
                           ln_g[l, 0], ln_b[l, 0])
            h2 = h.reshape(bsz * seq, d)
        else:
            lb = l - N_A_LAYERS
            h2 = _nsa_layer(h2, nsa_w_qg[lb], nsa_w_o[lb], shared, tabs, ovt, masks, ln_g[l, 0], ln_b[l, 0],
                            bsz, seq)
        if l % 2 == 0:
            h2 = _dense_ffn(h2, ffn_w_gu, ffn_w_down, l // 2, ln_g[l, 1], ln_b[l, 1])
        else:
            h2 = _moe(h2, moe_w_router[l // 2], moe_w_gu, moe_w_down, l // 2, ln_g[l, 1], ln_b[l, 1])
        h = h2.reshape(bsz, seq, d)
        if l == N_A_LAYERS - 1:
            shared = _shared_kv(h, nsa_w_kv, nsa_cmp_pos, nsa_cmp_w1, nsa_cmp_w2, tabs)
    return h
```

```python
import functools

import jax
import jax.numpy as jnp
from jax import lax
from jax.experimental import pallas as pl
from jax.experimental.pallas import tpu as pltpu

F32 = jnp.float32
BF16 = jnp.bfloat16
I32 = jnp.int32

DEPTH = 4
N_A_LAYERS = DEPTH // 2
LRU_BLOCKS = 8
CONV_WIDTH = 4
LRU_C = 8.0
N_HEADS = 16
N_KV = 4
HPG = N_HEADS // N_KV
HEAD_DIM = 64
ROT_DIM = HEAD_DIM // 4
ROPE_THETA = 500000.0
CMP_BLOCK = 32
CMP_STRIDE = 16
SEL_BLOCK = 64
N_SEL = 8
WINDOW = 256
N_EXPERTS = 8
DN_ALPHA = (2.0 * DEPTH) ** 0.25
LN_EPS = 1e-5
NEG = -1e30
BIG = 1e30
LOG2E = 1.4426950408889634

LANES = 128
SUBLANES = 8
VMEM_LIMIT = 52 * 1024 * 1024

LRU_TS = 512
FFN_TM = 1024
DENSE_TM = 512
DENSE_TF = 1408
ROUTE_TM = 512
PROJ_TM = 1024
ATT_TQ = 256
SLC_KC = 256
SEG_ALIGN = 16
VT_ROWS = 80


def _cparams(sem):
    return pltpu.CompilerParams(dimension_semantics=sem, vmem_limit_bytes=VMEM_LIMIT)


def _dot(a, b):
    return jnp.dot(a, b, preferred_element_type=F32)


def _split(a):
    hi = a.astype(BF16)
    lo = (a - hi.astype(F32)).astype(BF16)
    return hi, lo


def _dot_f32(a, b):
    a_hi, a_lo = _split(a)
    b_hi, b_lo = _split(b)
    n = b.shape[1]
    wide = _dot(a_hi, jnp.concatenate([b_hi, b_lo], axis=1))
    return wide[:, :n] + (wide[:, n:] + _dot(a_lo, b_hi))


def _sigmoid(x):
    return 0.5 * jnp.tanh(0.5 * x) + 0.5


def _postnorm(res, y, g, b):
    z = DN_ALPHA * res + y
    mu = jnp.mean(z, axis=-1, keepdims=True)
    zc = z - mu
    var = jnp.mean(zc * zc, axis=-1, keepdims=True)
    return zc * lax.rsqrt(var + LN_EPS) * g + b


def _lru_kernel(x_ref, win_ref, cw_ref, cb_ref, wai_ref, ba_ref, bi_ref, lam_ref, wout_ref,
                g_ref, b_ref, o_ref, hcar_ref, xprev_ref, hg_ref, *, ts, width, nblk):
    bw = width // nblk

    @pl.when(pl.program_id(1) == 0)
    def _():
        hcar_ref[...] = jnp.zeros_like(hcar_ref)
        xprev_ref[...] = jnp.zeros_like(xprev_ref)

    x = x_ref[0]
    u = _dot(x.astype(BF16), win_ref[...])
    xc = u[:, width:]
    xcat = jnp.concatenate([xprev_ref[...], xc], axis=0)
    xr = cb_ref[...] + cw_ref[CONV_WIDTH - 1:CONV_WIDTH, :] * xc
    for s in range(1, CONV_WIDTH):
        xr = xr + cw_ref[CONV_WIDTH - 1 - s:CONV_WIDTH - s, :] * xcat[SUBLANES - s:SUBLANES - s + ts]
    xprev_ref[...] = xc[ts - SUBLANES:ts]

    row = lax.broadcasted_iota(I32, (ts // SUBLANES, SUBLANES, bw), 1)
    for n in range(nblk):
        sl = slice(n * bw, (n + 1) * bw)
        xb = xr[:, sl]
        ri = _dot(xb.astype(BF16), wai_ref[n])
        r = _sigmoid(ri[:, :bw] + ba_ref[:, sl])
        i = _sigmoid(ri[:, bw:] + bi_ref[:, sl])
        lam = lam_ref[:, sl]
        sp = jnp.maximum(-lam, 0.0) + jnp.log(1.0 + jnp.exp(-jnp.abs(lam)))
        log_a = (-LRU_C * r) * sp
        a = jnp.exp(log_a)
        th = jnp.tanh(log_a)
        h = jnp.sqrt(-2.0 * th / (1.0 - th)) * (i * xb)
        a = a.reshape(ts // SUBLANES, SUBLANES, bw)
        h = h.reshape(ts // SUBLANES, SUBLANES, bw)
        d = 1
        while d < SUBLANES:
            keep = row >= d
            a_sh = jnp.where(keep, pltpu.roll(a, d, 1), 1.0)
            h_sh = jnp.where(keep, pltpu.roll(h, d, 1), 0.0)
            h = a * h_sh + h
            a = a * a_sh
            d *= 2
        carry = hcar_ref[:, sl]
        groups = []
        for v in range(ts // SUBLANES):
            hv = h[v] + a[v] * carry
            carry = hv[SUBLANES - 1:SUBLANES, :]
            groups.append(hv)
        h = jnp.concatenate(groups, axis=0)
        hcar_ref[:, sl] = carry
        gate = jax.nn.gelu(u[:, sl])
        hg_ref[:, sl] = (h * gate).astype(BF16)

    y = _dot(hg_ref[...], wout_ref[...])
    o_ref[0] = _postnorm(x, y, g_ref[...], b_ref[...])


def _lru_layer(x, w_in, conv_w, conv_b, w_a, b_a, w_i, b_i, lam, w_out, g, b):
    bsz, seq, d = x.shape
    width = w_out.shape[0]
    nblk = w_a.shape[0]
    ts = min(LRU_TS, seq)
    wai = jnp.concatenate([w_a, w_i], axis=-1).astype(BF16)
    row2 = lambda v: v.reshape(1, -1)
    const2 = lambda bi, ti: (0, 0)
    kern = functools.partial(_lru_kernel, ts=ts, width=width, nblk=nblk)
    return pl.pallas_call(
        kern,
        grid=(bsz, seq // ts),
        in_specs=[
            pl.BlockSpec((1, ts, d), lambda bi, ti: (bi, ti, 0)),
            pl.BlockSpec((d, 2 * width), const2),
            pl.BlockSpec((CONV_WIDTH, width), const2),
            pl.BlockSpec((1, width), const2),
            pl.BlockSpec(wai.shape, lambda bi, ti: (0, 0, 0)),
            pl.BlockSpec((1, width), const2),
            pl.BlockSpec((1, width), const2),
            pl.BlockSpec((1, width), const2),
            pl.BlockSpec((width, d), const2),
            pl.BlockSpec((1, d), const2),
            pl.BlockSpec((1, d), const2),
        ],
        out_specs=pl.BlockSpec((1, ts, d), lambda bi, ti: (bi, ti, 0)),
        out_shape=jax.ShapeDtypeStruct((bsz, seq, d), F32),
        scratch_shapes=[
            pltpu.VMEM((1, width), F32),
            pltpu.VMEM((SUBLANES, width), F32),
            pltpu.VMEM((ts, width), BF16),
        ],
        compiler_params=_cparams(("parallel", "arbitrary")),
        name="lru_layer",
    )(x, w_in.astype(BF16), conv_w, row2(conv_b), wai, row2(b_a), row2(b_i), row2(lam),
      w_out.astype(BF16), row2(g), row2(b))


def _ffn_kernel(te_ref, nu_ref, x_ref, wg_ref, wu_ref, wd_ref, g_ref, b_ref, o_ref, xb_ref, acc_ref,
                *, nf, postnorm):
    i = pl.program_id(0)
    f = pl.program_id(1)
    used = i < nu_ref[0]

    @pl.when(used)
    def _():
        @pl.when(f == 0)
        def _():
            xb_ref[...] = x_ref[...].astype(BF16)
            acc_ref[...] = jnp.zeros_like(acc_ref)

        xb = xb_ref[...]
        gp = _dot(xb, wg_ref[0, 0].astype(BF16))
        up = _dot(xb, wu_ref[0, 0].astype(BF16))
        act = (gp * jax.nn.sigmoid(gp) * up).astype(BF16)
        acc_ref[...] += _dot(act, wd_ref[0, 0].astype(BF16))

        @pl.when(f == nf - 1)
        def _():
            if postnorm:
                o_ref[...] = _postnorm(x_ref[...], acc_ref[...], g_ref[...], b_ref[...])
            else:
                o_ref[...] = acc_ref[...].astype(o_ref.dtype)

    @pl.when(jnp.logical_and(jnp.logical_not(used), f == nf - 1))
    def _():
        o_ref[...] = jnp.zeros_like(o_ref)


def _ffn(x, w_gu, w_down, layer, tile_expert, n_used, g, b, *, postnorm, tf, tm):
    rows, d = x.shape
    ff = w_down.shape[2]
    nf = ff // tf
    nt = rows // tm

    def fidx(i, f, nu):
        return jnp.where(i < nu[0], f, nf - 1)

    kern = functools.partial(_ffn_kernel, nf=nf, postnorm=postnorm)
    grid_spec = pltpu.PrefetchScalarGridSpec(
        num_scalar_prefetch=2,
        grid=(nt, nf),
        in_specs=[
            pl.BlockSpec((tm, d), lambda i, f, te, nu: (i, 0)),
            pl.BlockSpec((1, 1, d, tf), lambda i, f, te, nu: (layer, te[i], 0, fidx(i, f, nu))),
            pl.BlockSpec((1, 1, d, tf), lambda i, f, te, nu: (layer, te[i], 0, fidx(i, f, nu) + nf)),
            pl.BlockSpec((1, 1, tf, d), lambda i, f, te, nu: (layer, te[i], fidx(i, f, nu), 0)),
            pl.BlockSpec((1, d), lambda i, f, te, nu: (0, 0)),
            pl.BlockSpec((1, d), lambda i, f, te, nu: (0, 0)),
        ],
        out_specs=pl.BlockSpec((tm, d), lambda i, f, te, nu: (i, 0)),
        scratch_shapes=[pltpu.VMEM((tm, d), BF16), pltpu.VMEM((tm, d), F32)],
    )
    return pl.pallas_call(
        kern,
        grid_spec=grid_spec,
        out_shape=jax.ShapeDtypeStruct((rows, d), F32 if postnorm else BF16),
        compiler_params=_cparams(("parallel", "arbitrary")),
        name="ffn_postnorm" if postnorm else "ffn_experts",
    )(tile_expert, n_used, x, w_gu, w_gu, w_down, g.reshape(1, -1), b.reshape(1, -1))


def _dense_ffn(h2, w_gu, w_down, layer, g, b):
    rows = h2.shape[0]
    tm = min(DENSE_TM, rows)
    nt = rows // tm
    ff = w_down.shape[1]
    tf = DENSE_TF if ff % DENSE_TF == 0 else LANES
    return _ffn(h2, w_gu[:, None].astype(BF16), w_down[:, None].astype(BF16), layer,
                jnp.zeros((nt,), I32), jnp.full((1,), nt, I32), g, b, postnorm=True, tf=tf, tm=tm)


def _router_kernel(h_ref, wr_ref, ut_ref, xl_ref, mi_ref, mw_ref, seg_ref, *, tm, n_exp, loc):
    x = h_ref[...]
    logits = _dot_f32(x, wr_ref[...])
    lane = lax.broadcasted_iota(I32, (tm, LANES), 1)
    lg = jnp.where(lane < n_exp, logits, -jnp.inf)
    m0 = jnp.max(lg, axis=1, keepdims=True)
    i0 = jnp.min(jnp.where(lg == m0, lane, LANES), axis=1, keepdims=True)
    lg1 = jnp.where(lane == i0, -jnp.inf, lg)
    m1 = jnp.max(lg1, axis=1, keepdims=True)
    i1 = jnp.min(jnp.where(lg1 == m1, lane, LANES), axis=1, keepdims=True)
    e1 = jnp.exp(m1 - m0)
    w0 = 1.0 / (1.0 + e1)
    w1 = e1 / (1.0 + e1)

    oh0 = lane == i0
    oh1 = lane == i1
    oh = jnp.logical_or(oh0, oh1).astype(F32)
    tri = (lax.broadcasted_iota(I32, (tm, tm), 0) > lax.broadcasted_iota(I32, (tm, tm), 1)).astype(BF16)
    before = _dot(tri, oh.astype(BF16))
    cnt = jnp.sum(oh, axis=0, keepdims=True)
    padded = jnp.floor((cnt + (SEG_ALIGN - 1)) * (1.0 / SEG_ALIGN)) * SEG_ALIGN
    seg = _dot(jnp.broadcast_to(padded, (SUBLANES, LANES)).astype(BF16), ut_ref[...].astype(BF16))[0:1, :]
    lp0 = jnp.sum(jnp.where(oh0, seg + before, 0.0), axis=1, keepdims=True)
    lp1 = jnp.sum(jnp.where(oh1, seg + before, 0.0), axis=1, keepdims=True)
    lpt = jnp.where(lane == 0, lp0, jnp.where(lane == 1, lp1, -1.0)).T
    prow = lax.broadcasted_iota(I32, (loc, tm), 0).astype(F32)
    place = jnp.logical_or(prow == lpt[0:1, :], prow == lpt[1:2, :]).astype(BF16)
    xl_ref[0] = _dot(place, x.astype(BF16)).astype(BF16)

    mi_ref[...] = jnp.where(lane == 0, lp0, lp1).astype(I32)
    mw_ref[...] = jnp.where(lane == 0, w0, w1)
    row8 = lax.broadcasted_iota(I32, (SUBLANES, LANES), 0)
    seg_ref[0] = jnp.where(row8 == 0, cnt, seg).astype(I32)


def _router(h2, w_router):
    rows, d = h2.shape
    n_exp = w_router.shape[1]
    tm = min(ROUTE_TM, rows)
    nt = rows // tm
    loc = 2 * tm + LANES
    assert n_exp * (SEG_ALIGN - 1) <= LANES
    wr = jnp.pad(w_router, ((0, 0), (0, LANES - n_exp)))
    upper = (jnp.arange(LANES)[:, None] < jnp.arange(LANES)[None, :]).astype(F32)
    kern = functools.partial(_router_kernel, tm=tm, n_exp=n_exp, loc=loc)
    return pl.pallas_call(
        kern,
        grid=(nt,),
        in_specs=[pl.BlockSpec((tm, d), lambda i: (i, 0)), pl.BlockSpec((d, LANES), lambda i: (0, 0)),
                  pl.BlockSpec((LANES, LANES), lambda i: (0, 0))],
        out_specs=[
            pl.BlockSpec((1, loc, d), lambda i: (i, 0, 0)),
            pl.BlockSpec((tm, LANES), lambda i: (i, 0)),
            pl.BlockSpec((tm, LANES), lambda i: (i, 0)),
            pl.BlockSpec((1, SUBLANES, LANES), lambda i: (i, 0, 0)),
        ],
        out_shape=[
            jax.ShapeDtypeStruct((nt, loc, d), BF16),
            jax.ShapeDtypeStruct((rows, LANES), I32),
            jax.ShapeDtypeStruct((rows, LANES), F32),
            jax.ShapeDtypeStruct((nt, SUBLANES, LANES), I32),
        ],
        compiler_params=_cparams(("parallel",)),
        name="moe_router",
    )(h2, wr, upper)


def _segment_copies(loc_ref, srt_ref, units_ref, local_buf, sorted_hbm, sem, tile, n_exp, nbits, to_sorted, do):
    base = tile * n_exp
    for e in range(n_exp):
        lo = loc_ref[base + e]
        so = srt_ref[base + e]
        units = units_ref[base + e]
        off = 0
        for k in reversed(range(nbits)):
            size = SEG_ALIGN << k
            bit = (units >> k) & 1

            @pl.when(bit == 1)
            def _(off=off, size=size, lo=lo, so=so):
                local = local_buf.at[pl.ds(pl.multiple_of(lo + off, SEG_ALIGN), size)]
                srt = sorted_hbm.at[pl.ds(pl.multiple_of(so + off, SEG_ALIGN), size)]
                do(pltpu.make_async_copy(local, srt, sem) if to_sorted else pltpu.make_async_copy(srt, local, sem))

            off = off + bit * size


def _spread_kernel(loc_ref, srt_ref, units_ref, xl_ref, zin_ref, o_ref, sem, *, n_exp, nbits):
    del zin_ref
    for do in (lambda cp: cp.start(), lambda cp: cp.wait()):
        _segment_copies(loc_ref, srt_ref, units_ref, xl_ref.at[0], o_ref, sem, pl.program_id(0), n_exp, nbits,
                        True, do)


def _spread_segments(x_loc, seg_start, sorted_start, units, n_out, nbits):
    nt, loc, d = x_loc.shape
    n_exp = units.shape[1]
    kern = functools.partial(_spread_kernel, n_exp=n_exp, nbits=nbits)
    grid_spec = pltpu.PrefetchScalarGridSpec(
        num_scalar_prefetch=3,
        grid=(nt,),
        in_specs=[pl.BlockSpec((1, loc, d), lambda i, a, b, c: (i, 0, 0)), pl.BlockSpec(memory_space=pl.ANY)],
        out_specs=pl.BlockSpec(memory_space=pl.ANY),
        scratch_shapes=[pltpu.SemaphoreType.DMA(())],
    )
    return pl.pallas_call(
        kern,
        grid_spec=grid_spec,
        out_shape=jax.ShapeDtypeStruct((n_out, d), x_loc.dtype),
        input_output_aliases={4: 0},
        compiler_params=_cparams(("arbitrary",)),
        name="moe_spread",
    )(seg_start.reshape(-1), sorted_start.reshape(-1), units.reshape(-1), x_loc, jnp.zeros((n_out, d), x_loc.dtype))


def _combine_kernel(loc_ref, srt_ref, units_ref, mi_ref, mw_ref, h_ref, ys_ref, g_ref, b_ref, o_ref, yl_ref, sem,
                    *, tm, n_exp, nbits):
    loc = yl_ref.shape[1]
    i = pl.program_id(0)
    slot = i % 2

    def copies(tile, buf, do):
        _segment_copies(loc_ref, srt_ref, units_ref, yl_ref.at[buf], ys_ref, sem.at[buf], tile, n_exp, nbits,
                        False, do)

    @pl.when(i == 0)
    def _():
        yl_ref[...] = jnp.zeros_like(yl_ref)
        copies(0, 0, lambda cp: cp.start())

    @pl.when(i + 1 < pl.num_programs(0))
    def _():
        copies(i + 1, 1 - slot, lambda cp: cp.start())

    copies(i, slot, lambda cp: cp.wait())
    mi = mi_ref[...]
    w = mw_ref[...]
    col = lax.broadcasted_iota(I32, (tm, loc), 1)
    wmat = jnp.where(col == mi[:, 0:1], w[:, 0:1], 0.0) + jnp.where(col == mi[:, 1:2], w[:, 1:2], 0.0)
    w_hi, w_lo = _split(wmat)
    yl = yl_ref[slot]
    y = _dot(w_hi, yl) + _dot(w_lo, yl)
    o_ref[...] = _postnorm(h_ref[...], y, g_ref[...], b_ref[...])


def _combine(h2, ys, seg_start, sorted_start, units, loc, mi, mw, g, b, nbits):
    rows, d = h2.shape
    nt, n_exp = units.shape
    tm = rows // nt
    kern = functools.partial(_combine_kernel, tm=tm, n_exp=n_exp, nbits=nbits)
    grid_spec = pltpu.PrefetchScalarGridSpec(
        num_scalar_prefetch=3,
        grid=(nt,),
        in_specs=[
            pl.BlockSpec((tm, LANES), lambda i, a, b, c: (i, 0)),
            pl.BlockSpec((tm, LANES), lambda i, a, b, c: (i, 0)),
            pl.BlockSpec((tm, d), lambda i, a, b, c: (i, 0)),
            pl.BlockSpec(memory_space=pl.ANY),
            pl.BlockSpec((1, d), lambda i, a, b, c: (0, 0)),
            pl.BlockSpec((1, d), lambda i, a, b, c: (0, 0)),
        ],
        out_specs=pl.BlockSpec((tm, d), lambda i, a, b, c: (i, 0)),
        scratch_shapes=[pltpu.VMEM((2, loc, d), ys.dtype), pltpu.SemaphoreType.DMA((2,))],
    )
    return pl.pallas_call(
        kern,
        grid_spec=grid_spec,
        out_shape=jax.ShapeDtypeStruct((rows, d), F32),
        compiler_params=_cparams(("arbitrary",)),
        name="moe_combine",
    )(seg_start.reshape(-1), sorted_start.reshape(-1), units.reshape(-1), mi, mw, h2, ys,
      g.reshape(1, -1), b.reshape(1, -1))


def _moe(h2, w_router, w_gu, w_down, layer, g, b):
    rows, d = h2.shape
    n_exp = w_router.shape[1]
    tm = min(FFN_TM, rows)
    x_loc, mi, mw, seg = _router(h2, w_router)
    nt_r, loc, _ = x_loc.shape
    counts = seg[:, 0, :n_exp]
    seg_start = seg[:, 1, :n_exp]
    units = (counts + SEG_ALIGN - 1) // SEG_ALIGN
    seg_len = units * SEG_ALIGN
    group_rows = jnp.sum(seg_len, axis=0)
    tiles_per = (group_rows + tm - 1) // tm
    tile_end = jnp.cumsum(tiles_per)
    group_start = (tile_end - tiles_per) * tm
    sorted_start = group_start[None, :] + jnp.cumsum(seg_len, axis=0) - seg_len
    n_tiles = -(-(2 * rows + nt_r * n_exp * (SEG_ALIGN - 1)) // tm) + n_exp
    n_used = tile_end[-1]
    tid = jnp.minimum(jnp.arange(n_tiles, dtype=I32), n_used - 1)
    tile_expert = jnp.sum((tid[:, None] >= tile_end[None, :]).astype(I32), axis=1)
    tile_expert = jnp.minimum(tile_expert, n_exp - 1)
    nbits = (min(ROUTE_TM, rows) // SEG_ALIGN).bit_length()

    xs = _spread_segments(x_loc, seg_start, sorted_start, units, n_tiles * tm, nbits)
    ys = _ffn(xs, w_gu, w_down, layer, tile_expert, n_used.reshape(1), g, b, postnorm=False, tf=512, tm=tm)
    return _combine(h2, ys, seg_start, sorted_start, units, loc, mi, mw, g, b, nbits)


def _rope_cols(x, c, sa, sb):
    half = ROT_DIM // 2
    return x * c + pltpu.roll(x, LANES - half, 1) * sa + pltpu.roll(x, half, 1) * sb


def _rope_tables(seq):
    half = ROT_DIM // 2
    inv = ROPE_THETA ** (-jnp.arange(half, dtype=F32) / half)
    ang = jnp.arange(seq, dtype=F32)[:, None] * inv[None]
    cos, sin = jnp.cos(ang), jnp.sin(ang)
    ones = jnp.ones((seq, HEAD_DIM - ROT_DIM), F32)
    zeros = jnp.zeros((seq, HEAD_DIM - ROT_DIM), F32)
    zh = jnp.zeros((seq, half), F32)
    c = jnp.concatenate([cos, cos, ones], axis=1)
    sa = jnp.concatenate([-sin, zh, zeros], axis=1)
    sb = jnp.concatenate([zh, sin, zeros], axis=1)
    rep = LANES // HEAD_DIM
    return tuple(jnp.tile(t, (1, rep)) for t in (c, sa, sb))


def _kv_kernel(h_ref, w_ref, c_ref, sa_ref, sb_ref, raw_ref, ks_ref, vst_ref, kw_ref, vwt_ref, *, gw):
    res = _dot(h_ref[0].astype(BF16), w_ref[...])
    c, sa, sb = c_ref[...], sa_ref[...], sb_ref[...]
    ng = gw // HEAD_DIM

    def part(p):
        return res[:, p * gw:(p + 1) * gw]

    def roped(p):
        x = part(p)
        return jnp.concatenate(
            [_rope_cols(x[:, j * LANES:(j + 1) * LANES], c, sa, sb) for j in range(gw // LANES)], axis=1)

    for t in range(2):
        x = part(t)
        for g in range(ng):
            raw_ref[t, 0, g] = x[:, g * HEAD_DIM:(g + 1) * HEAD_DIM]
    tm = res.shape[0]
    pos = pl.program_id(1) * tm + lax.broadcasted_iota(I32, (tm, LANES - HEAD_DIM), 0)
    blk_onehot = (pos // SEL_BLOCK == lax.broadcasted_iota(I32, (tm, LANES - HEAD_DIM), 1)).astype(F32)
    ones_rows = (lax.broadcasted_iota(I32, (VT_ROWS - HEAD_DIM, tm), 0) == 0).astype(F32)
    for p, k_ref, vt_ref in ((2, ks_ref, vst_ref), (4, kw_ref, vwt_ref)):
        k = roped(p)
        vt = part(p + 1).T
        for g in range(ng):
            kg = k[:, g * HEAD_DIM:(g + 1) * HEAD_DIM]
            if p == 2:
                kg = jnp.concatenate([kg, blk_onehot], axis=1)
            k_ref[0, g] = kg.astype(BF16)
            vt_ref[0, g] = jnp.concatenate([vt[g * HEAD_DIM:(g + 1) * HEAD_DIM, :], ones_rows], axis=0).astype(BF16)


def _kv_proj(h3, w_kv, tabs):
    bsz, seq, d = h3.shape
    n = w_kv.shape[1]
    gw = n // 6
    ng = gw // HEAD_DIM
    tm = min(PROJ_TM, seq)
    assert seq // SEL_BLOCK <= LANES - HEAD_DIM
    tab_spec = pl.BlockSpec((tm, LANES), lambda b, i: (i, 0))

    def k_spec(w):
        return pl.BlockSpec((1, ng, tm, w), lambda b, i: (b, 0, i, 0))

    def k_shape(w):
        return jax.ShapeDtypeStruct((bsz, ng, seq, w), BF16)

    vt_spec = pl.BlockSpec((1, ng, VT_ROWS, tm), lambda b, i: (b, 0, 0, i))
    vt_shape = jax.ShapeDtypeStruct((bsz, ng, VT_ROWS, seq), BF16)
    kern = functools.partial(_kv_kernel, gw=gw)
    return pl.pallas_call(
        kern,
        grid=(bsz, seq // tm),
        in_specs=[pl.BlockSpec((1, tm, d), lambda b, i: (b, i, 0)), pl.BlockSpec((d, n), lambda b, i: (0, 0)),
                  tab_spec, tab_spec, tab_spec],
        out_specs=[pl.BlockSpec((2, 1, ng, tm, HEAD_DIM), lambda b, i: (0, b, 0, i, 0)),
                   k_spec(LANES), vt_spec, k_spec(HEAD_DIM), vt_spec],
        out_shape=[jax.ShapeDtypeStruct((2, bsz, ng, seq, HEAD_DIM), F32),
                   k_shape(LANES), vt_shape, k_shape(HEAD_DIM), vt_shape],
        compiler_params=_cparams(("parallel", "parallel")),
        name="kv_proj",
    )(h3, w_kv.astype(BF16), *tabs)


def _qg_kernel(h_ref, w_ref, c_ref, sa_ref, sb_ref, q_ref, qr_ref, gt_ref, *, nq):
    res = _dot(h_ref[0].astype(BF16), w_ref[...])
    c, sa, sb = c_ref[...], sa_ref[...], sb_ref[...]
    scale = HEAD_DIM ** -0.5 * LOG2E
    for j in range(nq // LANES):
        x = res[:, j * LANES:(j + 1) * LANES] * scale
        q_ref[0, :, j * LANES:(j + 1) * LANES] = x.astype(BF16)
        qr_ref[0, :, j * LANES:(j + 1) * LANES] = _rope_cols(x, c, sa, sb).astype(BF16)
    gt_ref[0] = jax.nn.sigmoid(res[:, nq:])


def _qg_proj(h3, w_qg, tabs):
    bsz, seq, d = h3.shape
    nq = N_HEADS * HEAD_DIM
    ngate = HPG * 3
    tm = min(PROJ_TM, seq)
    wg = w_qg[:, nq:].reshape(d, N_KV, ngate)
    wg = jnp.pad(wg, ((0, 0), (0, 0), (0, LANES - ngate))).reshape(d, N_KV * LANES)
    w = jnp.concatenate([w_qg[:, :nq], wg], axis=1).astype(BF16)
    n = w.shape[1]
    tab_spec = pl.BlockSpec((tm, LANES), lambda b, i: (i, 0))
    kern = functools.partial(_qg_kernel, nq=nq)
    return pl.pallas_call(
        kern,
        grid=(bsz, seq // tm),
        in_specs=[pl.BlockSpec((1, tm, d), lambda b, i: (b, i, 0)), pl.BlockSpec((d, n), lambda b, i: (0, 0)),
                  tab_spec, tab_spec, tab_spec],
        out_specs=[pl.BlockSpec((1, tm, nq), lambda b, i: (b, i, 0)), pl.BlockSpec((1, tm, nq), lambda b, i: (b, i, 0)),
                   pl.BlockSpec((1, tm, N_KV * LANES), lambda b, i: (b, i, 0))],
        out_shape=[jax.ShapeDtypeStruct((bsz, seq, nq), BF16), jax.ShapeDtypeStruct((bsz, seq, nq), BF16),
                   jax.ShapeDtypeStruct((bsz, seq, N_KV * LANES), F32)],
        compiler_params=_cparams(("parallel", "parallel")),
        name="qg_proj",
    )(h3, w, *tabs)


def _mm_ln_kernel(a_ref, w_ref, res_ref, g_ref, b_ref, o_ref):
    half = a_ref.shape[0] // 2
    for r in (slice(0, half), slice(half, 2 * half)):
        y = _dot(a_ref[r, :], w_ref[...])
        o_ref[r, :] = _postnorm(res_ref[r, :], y, g_ref[...], b_ref[...])


def _mm_postnorm(a, w, res, g, b):
    rows, k = a.shape
    d = w.shape[1]
    tm = min(PROJ_TM, rows)
    return pl.pallas_call(
        _mm_ln_kernel,
        grid=(rows // tm,),
        in_specs=[pl.BlockSpec((tm, k), lambda i: (i, 0)), pl.BlockSpec((k, d), lambda i: (0, 0)),
                  pl.BlockSpec((tm, d), lambda i: (i, 0)),
                  pl.BlockSpec((1, d), lambda i: (0, 0)), pl.BlockSpec((1, d), lambda i: (0, 0))],
        out_specs=pl.BlockSpec((tm, d), lambda i: (i, 0)),
        out_shape=jax.ShapeDtypeStruct((rows, d), F32),
        compiler_params=_cparams(("parallel",)),
        name="out_proj_postnorm",
    )(a, w.astype(BF16), res, g.reshape(1, -1), b.reshape(1, -1))


def _compress_kernel(raw_ref, w1_ref, pe_ref, w2_ref, o_ref, ot_ref, *, rows_per):
    half = w1_ref.shape[1] // 2
    pieces = [raw_ref[0, 0, 0, pl.ds(l, rows_per, stride=CMP_STRIDE), :] for l in range(CMP_STRIDE)]
    x = jnp.concatenate(pieces, axis=1).astype(BF16)
    first = _dot(x, w1_ref[0, :half, :])
    second = _dot(x, w1_ref[0, half:, :])
    bias = _dot(pe_ref[0].astype(BF16), w1_ref[0])[0:1, :]
    hid = first + pltpu.roll(second, rows_per - 1, 0) + bias
    out = _dot(jax.nn.gelu(hid).astype(BF16), w2_ref[0])
    o_ref[0, 0, 0] = out.astype(BF16)
    wide = jnp.concatenate([out, jnp.zeros_like(out)], axis=1)
    ot_ref[0, 0, 0] = wide.T[:out.shape[1], :].astype(BF16)


def _compress(raw, w1, pe, w2):
    _, bsz, ng, seq, dh = raw.shape
    rows_per = seq // CMP_STRIDE
    k2 = w1.shape[1]
    hid = w1.shape[2]
    pe8 = jnp.broadcast_to(pe.reshape(2, 1, k2), (2, SUBLANES, k2))
    kern = functools.partial(_compress_kernel, rows_per=rows_per)
    return pl.pallas_call(
        kern,
        grid=(2, bsz, ng),
        in_specs=[
            pl.BlockSpec((1, 1, 1, seq, dh), lambda t, b, g: (t, b, g, 0, 0)),
            pl.BlockSpec((1, k2, hid), lambda t, b, g: (t, 0, 0)),
            pl.BlockSpec((1, SUBLANES, k2), lambda t, b, g: (t, 0, 0)),
            pl.BlockSpec((1, hid, dh), lambda t, b, g: (t, 0, 0)),
        ],
        out_specs=[pl.BlockSpec((1, 1, 1, rows_per, dh), lambda t, b, g: (t, b, g, 0, 0)),
                   pl.BlockSpec((1, 1, 1, dh, rows_per), lambda t, b, g: (t, b, g, 0, 0))],
        out_shape=[jax.ShapeDtypeStruct((2, bsz, ng, rows_per, dh), BF16),
                   jax.ShapeDtypeStruct((2, bsz, ng, dh, rows_per), BF16)],
        compiler_params=_cparams(("parallel", "parallel", "parallel")),
        name="kv_compress",
    )(raw, w1.astype(BF16), pe8, w2.astype(BF16))


def _softmax2_cols(s):
    m = jnp.max(s, axis=0, keepdims=True)
    e = jnp.exp2(s - m)
    return e * (1.0 / jnp.sum(e, axis=0, keepdims=True))


def _heads_t(x):
    xt = x.astype(F32).T
    return jnp.concatenate([xt[h * HEAD_DIM:(h + 1) * HEAD_DIM, :] for h in range(HPG)], axis=1)


def _attn_kernel(q_ref, qr_ref, gt_ref, kc_ref, vct_ref, ks_ref, vst_ref, kw_ref, vwt_ref, ovt_ref,
                 diagb_ref, winb_ref, o_ref, s0_ref, s1_ref, e0_ref, e1_ref, *, tq, nblk):
    nq = HPG * tq
    it = pl.program_id(2)
    t0 = it * tq
    lane_q = lax.broadcasted_iota(I32, (1, nq), 1)
    qpos = t0 + lane_q % tq
    qt = _heads_t(q_ref[0]).astype(BF16)
    qrt = _heads_t(qr_ref[0]).astype(BF16)

    wk = WINDOW + tq
    nwin = WINDOW // tq
    k0 = pl.multiple_of(jnp.maximum(t0 - WINDOW, 0), tq)
    s = _dot(kw_ref[0, 0, pl.ds(k0, wk), :], qrt) + winb_ref[jnp.minimum(it, nwin)]
    ew = jnp.exp2(s - jnp.max(s, axis=0, keepdims=True))
    ow = _dot(vwt_ref[0, 0, :, pl.ds(k0, wk)], ew.astype(BF16))
    o_win = ow[:HEAD_DIM] * (1.0 / ow[HEAD_DIM:HEAD_DIM + 1])

    ncp = kc_ref.shape[3]
    sc = _dot(kc_ref[0, 0, 0], qt)
    blk_end = lax.broadcasted_iota(I32, (ncp, 1), 0) * CMP_STRIDE + (CMP_BLOCK - 1)
    p = _softmax2_cols(jnp.where(blk_end <= qpos, sc, NEG))
    p = p * (qpos >= CMP_BLOCK - 1).astype(F32)
    o_cmp = _dot(vct_ref[0, 0, 0], p.astype(BF16))

    psum = p[:, 0:tq]
    for hh in range(1, HPG):
        psum = psum + p[:, hh * tq:(hh + 1) * tq]
    p_hi, p_lo = _split(psum)
    ovt = ovt_ref[...].astype(BF16)
    imp = _dot(ovt, p_hi) + _dot(ovt, p_lo)
    jj = lax.broadcasted_iota(I32, (nblk, 1), 0)
    qp1 = t0 + lax.broadcasted_iota(I32, (1, tq), 1)
    cur = qp1 // SEL_BLOCK
    forced = jnp.logical_or(jj == 0, jnp.logical_or(jj == cur, jj == cur - 1))
    score = jnp.where(forced, BIG, jnp.where(jj * SEL_BLOCK <= qp1, imp, NEG))
    ahead = jnp.zeros((nblk, tq), F32)
    for jp in range(nblk):
        sj = score[jp:jp + 1, :]
        first = jnp.logical_or(sj > score, jnp.logical_and(sj == score, jp < jj))
        ahead = ahead + first.astype(F32)
    selb = jnp.where(ahead < float(min(N_SEL, nblk)), 0.0, NEG)
    selb = jnp.concatenate([selb] * HPG, axis=1)
    pad = jnp.zeros((LANES - HEAD_DIM - nblk, nq), F32)
    qx = jnp.concatenate([qrt.astype(F32), selb, pad], axis=0).astype(BF16)

    kc = SLC_KC
    n = t0 // kc + 1
    kd = pl.multiple_of((n - 1) * kc, kc)
    sbufs = (s0_ref, s1_ref)
    ebufs = (e0_ref, e1_ref)
    s0_ref[...] = _dot(ks_ref[0, 0, pl.ds(kd, kc), :], qx) + diagb_ref[(t0 - kd) // tq]
    e1_ref[...] = jnp.zeros((kc, nq), BF16)

    def chunk_start(c):
        return pl.multiple_of(jnp.where(c <= 0, kd, (c - 1) * kc), kc)

    def pv(c, e_ref):
        return _dot(vst_ref[0, 0, :, pl.ds(chunk_start(c), kc)], e_ref[...])

    def step(i, par, carry):
        m, acc, alpha_p = carry
        acc = alpha_p * acc + pv(i - 2, ebufs[par])
        sp = sbufs[1 - par][...]
        m_new = jnp.maximum(m, jnp.max(sp, axis=0, keepdims=True))
        alpha = jnp.exp2(m - m_new)
        ebufs[1 - par][...] = jnp.exp2(sp - m_new).astype(BF16)
        k0 = chunk_start(jnp.minimum(i, n - 1))
        sbufs[par][...] = _dot(ks_ref[0, 0, pl.ds(k0, kc), :], qx)
        return m_new, acc, alpha

    def pair(p, carry):
        return step(2 * p + 2, 0, step(2 * p + 1, 1, carry))

    init = (jnp.full((1, nq), NEG, F32), jnp.zeros((VT_ROWS, nq), F32), jnp.ones((1, nq), F32))
    carry = lax.fori_loop(0, n // 2, pair, init)
    odd = n % 2 == 1
    _, acc, alpha_p = lax.cond(odd, lambda c: step(n, 1, c), lambda c: c, carry)
    acc = alpha_p * acc + jnp.where(odd, pv(n - 1, e0_ref), pv(n - 1, e1_ref))
    o_slc = acc[:HEAD_DIM] * (1.0 / acc[HEAD_DIM:HEAD_DIM + 1])

    gtt = gt_ref[0].T

    def gate(k):
        return jnp.concatenate([gtt[h * 3 + k:h * 3 + k + 1, :] for h in range(HPG)], axis=1)

    o = gate(0) * o_cmp + gate(1) * o_slc + gate(2) * o_win
    o = jnp.concatenate([o[:, h * tq:(h + 1) * tq] for h in range(HPG)], axis=0)
    o_ref[0] = o.T.astype(BF16)


def _attn_masks(tq):
    nq = HPG * tq
    lq = (jnp.arange(nq) % tq)[None, :]
    kr = jnp.arange(SLC_KC)[:, None]
    diag = jnp.stack([jnp.where(kr <= d * tq + lq, 0.0, NEG) for d in range(max(SLC_KC // tq, 1))])
    kr = jnp.arange(WINDOW + tq)[:, None]
    win = []
    for w in range(WINDOW // tq + 1):
        delta = w * tq + lq - kr
        win.append(jnp.where((delta >= 0) & (delta < WINDOW), 0.0, NEG))
    return diag.astype(F32), jnp.stack(win).astype(F32)


def _nsa_attention(q, qr, gates, cmp, cmp_t, ks, vst, kw, vwt, ovt, masks):
    bsz, seq, _ = q.shape
    ng, dh = kw.shape[1], kw.shape[3]
    tq = min(ATT_TQ, seq)
    nq = HPG * tq
    nblk = seq // SEL_BLOCK
    ncp = cmp.shape[3]
    diagb, winb = masks
    qspec = pl.BlockSpec((1, tq, HPG * dh), lambda b, g, i: (b, i, g))
    kern = functools.partial(_attn_kernel, tq=tq, nblk=nblk)
    return pl.pallas_call(
        kern,
        grid=(bsz, ng, seq // tq),
        in_specs=[
            qspec, qspec,
            pl.BlockSpec((1, tq, LANES), lambda b, g, i: (b, i, g)),
            pl.BlockSpec((1, 1, 1, ncp, dh), lambda b, g, i: (0, b, g, 0, 0)),
            pl.BlockSpec((1, 1, 1, dh, ncp), lambda b, g, i: (1, b, g, 0, 0)),
            pl.BlockSpec((1, 1, seq, LANES), lambda b, g, i: (b, g, 0, 0)),
            pl.BlockSpec((1, 1, VT_ROWS, seq), lambda b, g, i: (b, g, 0, 0)),
            pl.BlockSpec((1, 1, seq, dh), lambda b, g, i: (b, g, 0, 0)),
            pl.BlockSpec((1, 1, VT_ROWS, seq), lambda b, g, i: (b, g, 0, 0)),
            pl.BlockSpec((nblk, ncp), lambda b, g, i: (0, 0)),
            pl.BlockSpec(diagb.shape, lambda b, g, i: (0, 0, 0)),
            pl.BlockSpec(winb.shape, lambda b, g, i: (0, 0, 0)),
        ],
        out_specs=qspec,
        out_shape=jax.ShapeDtypeStruct((bsz, seq, ng * HPG * dh), BF16),
        scratch_shapes=[pltpu.VMEM((SLC_KC, nq), F32), pltpu.VMEM((SLC_KC, nq), F32),
                        pltpu.VMEM((SLC_KC, nq), BF16), pltpu.VMEM((SLC_KC, nq), BF16)],
        compiler_params=_cparams(("parallel", "parallel", "arbitrary")),
        name="nsa_attention",
    )(q, qr, gates, cmp, cmp_t, ks, vst, kw, vwt, ovt, diagb, winb)


def _overlap_t(seq):
    ncp = seq // CMP_STRIDE
    nblk = seq // SEL_BLOCK
    c_start = jnp.arange(ncp) * CMP_STRIDE
    j_start = jnp.arange(nblk) * SEL_BLOCK
    ov = (c_start[None, :] < j_start[:, None] + SEL_BLOCK) & (c_start[None, :] + CMP_BLOCK > j_start[:, None])
    return ov.astype(F32)


def _shared_kv(h3, w_kv, cmp_pos, cmp_w1, cmp_w2, tabs):
    raw, ks, vst, kw, vwt = _kv_proj(h3, w_kv, tabs)
    cmp, cmp_t = _compress(raw, cmp_w1, cmp_pos, cmp_w2)
    return cmp, cmp_t, ks, vst, kw, vwt


def _nsa_layer(h2, w_qg, w_o, shared, tabs, ovt, masks, g, b, bsz, seq):
    q, qr, gates = _qg_proj(h2.reshape(bsz, seq, -1), w_qg, tabs)
    o = _nsa_attention(q, qr, gates, *shared, ovt, masks)
    return _mm_postnorm(o.reshape(bsz * seq, -1), w_o, h2, g, b)


def kernel(x, ln_g, ln_b, lru_w_in, lru_conv_w, lru_conv_b, lru_w_a, lru_b_a, lru_w_i, lru_b_i, lru_lambda,
           lru_w_out, nsa_w_kv, nsa_cmp_pos, nsa_cmp_w1, nsa_cmp_w2, nsa_w_qg, nsa_w_o, ffn_w_gu, ffn_w_down,
           moe_w_router, moe_w_gu, moe_w_down):
    bsz, seq, d = x.shape
    tabs = _rope_tables(seq)
    ovt = _overlap_t(seq)
    masks = _attn_masks(min(ATT_TQ, seq))
    h = x
    shared = None
    for l in range(DEPTH):
        if l < N_A_LAYERS:
            h = _lru_layer(h, lru_w_in[l], lru_conv_w[l], lru_conv_b[l], lru_w_a[l], lru_b_a[l].reshape(-1),
                           lru_w_i[l], lru_b_i[l].reshape(-1), lru_lambda[l], lru_w_out[l],
                           ln_g[l, 0], ln_b[l, 0])
            h2 = h.reshape(bsz * seq, d)
        else:
            lb = l - N_A_LAYERS
            h2 = _nsa_layer(h2, nsa_w_qg[lb], nsa_w_o[lb], shared, tabs, ovt, masks, ln_g[l, 0], ln_b[l, 0],
                            bsz, seq)
        if l % 2 == 0:
            h2 = _dense_ffn(h2, ffn_w_gu, ffn_w_down, l // 2, ln_g[l, 1], ln_b[l, 1])
        else:
            h2 = _moe(h2, moe_w_router[l // 2], moe_w_gu, moe_w_down, l // 2, ln_g[l, 1], ln_b[l, 1])
        h = h2.reshape(bsz, seq, d)
        if l == N_A_LAYERS - 1:
            shared = _shared_kv(h, nsa_w_kv, nsa_cmp_pos, nsa_cmp_w1, nsa_cmp_w2, tabs)
    return h
```
